```python
import math, functools
import jax, jax.numpy as jnp
from jax import lax
import numpy as np

D_MODEL = 1024
BATCH = 2
SEQ = 8192
DEPTH = 1
DEC_BATCH = 128
DEC_SEQ = 8
PAST_LEN = 8192
PAGE_SIZE = 128

MIX_WIDTH = D_MODEL
MLSTM_HEADS = 4
MLSTM_HEAD_DIM = (MIX_WIDTH // 2) // MLSTM_HEADS
MLSTM_WIDTH = MLSTM_HEADS * MLSTM_HEAD_DIM
FOX_HEADS = 8
FOX_HEAD_DIM = (MIX_WIDTH - MLSTM_WIDTH) // FOX_HEADS
FOX_WIDTH = FOX_HEADS * FOX_HEAD_DIM
QK_CONV = 4
FFN_CONV = 3
D_FF = ((8 * D_MODEL // 3 + 127) // 128) * 128
MLSTM_CHUNK = 128
Q_BLOCK = 128
EPS = 1e-6
NEG_INF = -1e30

OFF_QM = 0
OFF_KM = MLSTM_WIDTH
OFF_VM = 2 * MLSTM_WIDTH
OFF_OM = 3 * MLSTM_WIDTH
OFF_QF = 4 * MLSTM_WIDTH
OFF_KF = OFF_QF + FOX_WIDTH
OFF_VF = OFF_QF + 2 * FOX_WIDTH
OFF_OF = OFF_QF + 3 * FOX_WIDTH
OFF_GATE = OFF_QF + 4 * FOX_WIDTH
N_GATES = 2 * MLSTM_HEADS + FOX_HEADS
PROJ_COLS = OFF_GATE + N_GATES

kernel_name = 'hymba_mlstm_fox_convffn_step'


def rms_norm(x, g):
    xf = x.astype(jnp.float32)
    return xf * lax.rsqrt(jnp.mean(xf * xf, axis=-1, keepdims=True) + EPS) * g.astype(jnp.float32)


def causal_dwconv(x, buf, w):
    t = x.shape[1]
    xp = jnp.concatenate([buf.astype(jnp.float32), x.astype(jnp.float32)], axis=1)
    wf = w.astype(jnp.float32)
    y = xp[:, 0:t] * wf[0]
    for j in range(1, w.shape[0]):
        y = y + xp[:, j:j + t] * wf[j]
    return y, xp[:, t:]


def mixer_inputs(xn, w_in, b_gate, w_qkconv, g_qf, g_kf, qk_buf):
    b, t, _ = xn.shape
    p = jnp.einsum('btd,dc->btc', xn, w_in.astype(jnp.float32))
    qk, new_qk = causal_dwconv(p[..., OFF_QM:OFF_VM], qk_buf, w_qkconv)
    qk = jax.nn.silu(qk)
    q_m = qk[..., :MLSTM_WIDTH].reshape(b, t, MLSTM_HEADS, MLSTM_HEAD_DIM)
    k_m = qk[..., MLSTM_WIDTH:].reshape(b, t, MLSTM_HEADS, MLSTM_HEAD_DIM) * (MLSTM_HEAD_DIM ** -0.5)
    v_m = p[..., OFF_VM:OFF_OM].reshape(b, t, MLSTM_HEADS, MLSTM_HEAD_DIM)
    o_m = jax.nn.sigmoid(p[..., OFF_OM:OFF_QF])
    q_f = rms_norm(p[..., OFF_QF:OFF_KF].reshape(b, t, FOX_HEADS, FOX_HEAD_DIM), g_qf)
    k_f = rms_norm(p[..., OFF_KF:OFF_VF].reshape(b, t, FOX_HEADS, FOX_HEAD_DIM), g_kf)
    v_f = p[..., OFF_VF:OFF_OF].reshape(b, t, FOX_HEADS, FOX_HEAD_DIM)
    o_f = jax.nn.sigmoid(p[..., OFF_OF:OFF_GATE])
    gates = p[..., OFF_GATE:] + b_gate.astype(jnp.float32)
    i_m = gates[..., :MLSTM_HEADS]
    logf_m = jax.nn.log_sigmoid(gates[..., MLSTM_HEADS:2 * MLSTM_HEADS])
    logf_f = jax.nn.log_sigmoid(gates[..., 2 * MLSTM_HEADS:])
    return (q_m, k_m, v_m, o_m, i_m, logf_m, q_f, k_f, v_f, o_f, logf_f, new_qk)


def mlstm_chunkwise(q, k, v, ig, logf, c0, n0, m0):
    b, t, h, d = q.shape
    L = MLSTM_CHUNK if t % MLSTM_CHUNK == 0 else t
    nc = t // L

    def chunks4(a):
        return a.reshape(b, nc, L, h, d).transpose(1, 0, 3, 2, 4)

    def chunks3(a):
        return a.reshape(b, nc, L, h).transpose(1, 0, 3, 2)

    causal = jnp.tril(jnp.ones((L, L), dtype=bool))

    def step(carry, xs):
        c, n, m = carry
        qc, kc, vc, ic, fc = xs
        bcum = jnp.cumsum(fc, axis=-1)
        log_d = jnp.where(causal, bcum[..., :, None] - bcum[..., None, :] + ic[..., None, :], -jnp.inf)
        inter = bcum + m[..., None]
        m_t = jnp.maximum(inter, jnp.max(log_d, axis=-1))
        s = jnp.einsum('bhtd,bhsd->bhts', qc, kc) * jnp.exp(log_d - m_t[..., None])
        sc = jnp.exp(inter - m_t)
        num = jnp.einsum('bhts,bhsd->bhtd', s, vc) + sc[..., None] * jnp.einsum('bhvk,bhtk->bhtv', c, qc)
        den = jnp.sum(s, axis=-1) + sc * jnp.einsum('bhk,bhtk->bht', n, qc)
        hc = num / jnp.maximum(jnp.abs(den), jnp.exp(-m_t))[..., None]
        m_new = m_t[..., -1]
        w = jnp.exp(bcum[..., -1:] - bcum + ic - m_new[..., None])
        decay = jnp.exp(bcum[..., -1] + m - m_new)
        c_new = decay[..., None, None] * c + jnp.einsum('bhs,bhsv,bhsk->bhvk', w, vc, kc)
        n_new = decay[..., None] * n + jnp.einsum('bhs,bhsk->bhk', w, kc)
        return (c_new, n_new, m_new), hc

    carry0 = (c0.astype(jnp.float32), n0.astype(jnp.float32), m0.astype(jnp.float32))
    (c, n, m), hs = lax.scan(step, carry0, (chunks4(q), chunks4(k), chunks4(v), chunks3(ig), chunks3(logf)))
    return hs.transpose(1, 0, 3, 2, 4).reshape(b, t, h, d), c, n, m


def fox_prompt(q, k, v, logf):
    b, t, h, d = q.shape
    nb = t // Q_BLOCK
    scale = d ** -0.5
    f = jnp.cumsum(logf, axis=1).transpose(0, 2, 1)
    k_pos = jnp.arange(t)
    q_blocks = q.reshape(b, nb, Q_BLOCK, h, d).transpose(1, 0, 2, 3, 4)
    f_blocks = f.reshape(b, h, nb, Q_BLOCK).transpose(2, 0, 1, 3)
    q_pos = k_pos.reshape(nb, Q_BLOCK)

    def block(args):
        qb, fb, qp = args
        logits = jnp.einsum('bqhd,bkhd->bhqk', qb, k) * scale + fb[..., None] - f[:, :, None, :]
        logits = jnp.where(k_pos[None, :] <= qp[:, None], logits, NEG_INF)
        prob = jax.nn.softmax(logits, axis=-1)
        return jnp.einsum('bhqk,bkhd->bqhd', prob, v)

    out = lax.map(block, (q_blocks, f_blocks, q_pos))
    return out.transpose(1, 0, 2, 3, 4).reshape(b, t, h, d)


def fox_sample(q, k, v, logf, k_past, v_past, logf_past):
    t = q.shape[1]
    p_len = k_past.shape[1]
    scale = FOX_HEAD_DIM ** -0.5
    f_past = jnp.cumsum(logf_past.astype(jnp.float32), axis=1)
    f_new = f_past[:, -1:] + jnp.cumsum(logf, axis=1)
    fq = f_new.transpose(0, 2, 1)
    lp = (jnp.einsum('bqhd,bkhd->bhqk', q, k_past.astype(jnp.float32)) * scale
          + fq[..., None] - f_past.transpose(0, 2, 1)[:, :, None, :])
    causal = jnp.tril(jnp.ones((t, t), dtype=bool))
    ln = jnp.einsum('bqhd,bkhd->bhqk', q, k) * scale + fq[..., None] - fq[:, :, None, :]
    ln = jnp.where(causal, ln, NEG_INF)
    prob = jax.nn.softmax(jnp.concatenate([lp, ln], axis=-1), axis=-1)
    return (jnp.einsum('bhqk,bkhd->bqhd', prob[..., :p_len], v_past.astype(jnp.float32))
            + jnp.einsum('bhqk,bkhd->bqhd', prob[..., p_len:], v))


def layer_step(x, qk_buf, ffn_buf, c0, n0, m0, fox_fn, g_mix, w_in, b_gate, w_qkconv, g_qf, g_kf,
               g_mh, w_out, g_ffn, w_up, w_ffconv, b_ffconv, w_down):
    b, t, _ = x.shape
    xn = rms_norm(x, g_mix)
    (q_m, k_m, v_m, o_m, i_m, logf_m, q_f, k_f, v_f, o_f, logf_f, new_qk) = mixer_inputs(
        xn, w_in, b_gate, w_qkconv, g_qf, g_kf, qk_buf)
    h_m, c, n, m = mlstm_chunkwise(q_m, k_m, v_m, i_m, logf_m, c0, n0, m0)
    a_f = fox_fn(q_f, k_f, v_f, logf_f)
    out_m = o_m * rms_norm(h_m, g_mh.reshape(MLSTM_HEADS, MLSTM_HEAD_DIM)).reshape(b, t, MLSTM_WIDTH)
    out_f = o_f * a_f.reshape(b, t, FOX_WIDTH)
    mix = jnp.einsum('btc,cd->btd', jnp.concatenate([out_m, out_f], axis=-1), w_out.astype(jnp.float32))
    h = x.astype(jnp.float32) + mix
    hn = rms_norm(h, g_ffn)
    up = jnp.einsum('btd,df->btf', hn, w_up.astype(jnp.float32))
    a, new_ffn = causal_dwconv(up[..., :D_FF], ffn_buf, w_ffconv)
    act = jax.nn.silu(a + b_ffconv.astype(jnp.float32)) * up[..., D_FF:]
    y = h + jnp.einsum('btf,fd->btd', act, w_down.astype(jnp.float32))
    return y, (k_f, v_f, logf_f, c, n, m, new_qk, new_ffn)


def _stack(states, i):
    return jnp.stack([s[i] for s in states], axis=0)


def setup_inputs(seed: int = 0) -> dict:
    key = jax.random.key(seed)
    ks = jax.random.split(key, 26)
    nrm = jax.random.normal
    n_pages = PAST_LEN // PAGE_SIZE
    n_used = DEC_BATCH * n_pages
    n_phys = n_used + max(1, n_used // 4)
    page_table = jax.random.permutation(ks[0], n_phys)[:n_used].reshape(DEC_BATCH, n_pages).astype(jnp.int32)
    x_prompt = nrm(ks[1], (BATCH, SEQ, D_MODEL), jnp.float32)
    x_sample = nrm(ks[2], (DEC_BATCH, DEC_SEQ, D_MODEL), jnp.float32)
    cache_k = nrm(ks[3], (DEPTH, n_phys, PAGE_SIZE, FOX_HEADS, FOX_HEAD_DIM), jnp.float32)
    cache_v = nrm(ks[4], (DEPTH, n_phys, PAGE_SIZE, FOX_HEADS, FOX_HEAD_DIM), jnp.float32)
    cache_logf = jax.nn.log_sigmoid(4.0 + 0.5 * nrm(ks[5], (DEPTH, n_phys, PAGE_SIZE, FOX_HEADS), jnp.float32))
    state_C = nrm(ks[6], (DEPTH, DEC_BATCH, MLSTM_HEADS, MLSTM_HEAD_DIM, MLSTM_HEAD_DIM), jnp.float32) * MLSTM_HEAD_DIM ** -0.5
    state_n = nrm(ks[7], (DEPTH, DEC_BATCH, MLSTM_HEADS, MLSTM_HEAD_DIM), jnp.float32) * MLSTM_HEAD_DIM ** -0.5
    state_m = nrm(ks[8], (DEPTH, DEC_BATCH, MLSTM_HEADS), jnp.float32)
    state_qkconv = nrm(ks[9], (DEPTH, DEC_BATCH, QK_CONV - 1, 2 * MLSTM_WIDTH), jnp.float32)
    state_ffnconv = nrm(ks[10], (DEPTH, DEC_BATCH, FFN_CONV - 1, D_FF), jnp.float32)
    g_mix = 1.0 + 0.05 * nrm(ks[11], (DEPTH, D_MODEL), jnp.float32)
    w_in = nrm(ks[12], (DEPTH, D_MODEL, PROJ_COLS), jnp.float32) * D_MODEL ** -0.5
    b_i = 0.1 * nrm(ks[13], (DEPTH, MLSTM_HEADS), jnp.float32)
    b_fm = jnp.linspace(3.0, 6.0, MLSTM_HEADS, dtype=jnp.float32) + 0.1 * nrm(ks[14], (DEPTH, MLSTM_HEADS), jnp.float32)
    b_ff = jnp.linspace(3.0, 6.0, FOX_HEADS, dtype=jnp.float32) + 0.1 * nrm(ks[15], (DEPTH, FOX_HEADS), jnp.float32)
    b_gate = jnp.concatenate([b_i, b_fm, b_ff], axis=-1)
    w_qkconv = nrm(ks[16], (DEPTH, QK_CONV, 2 * MLSTM_WIDTH), jnp.float32) * QK_CONV ** -0.5
    g_qf = 1.0 + 0.05 * nrm(ks[17], (DEPTH, FOX_HEAD_DIM), jnp.float32)
    g_kf = 1.0 + 0.05 * nrm(ks[18], (DEPTH, FOX_HEAD_DIM), jnp.float32)
    g_mh = 1.0 + 0.05 * nrm(ks[19], (DEPTH, MLSTM_WIDTH), jnp.float32)
    w_out = nrm(ks[20], (DEPTH, MIX_WIDTH, D_MODEL), jnp.float32) * MIX_WIDTH ** -0.5
    g_ffn = 1.0 + 0.05 * nrm(ks[21], (DEPTH, D_MODEL), jnp.float32)
    w_up = nrm(ks[22], (DEPTH, D_MODEL, 2 * D_FF), jnp.float32) * D_MODEL ** -0.5
    w_ffconv = nrm(ks[23], (DEPTH, FFN_CONV, D_FF), jnp.float32) * FFN_CONV ** -0.5
    b_ffconv = 0.02 * nrm(ks[24], (DEPTH, D_FF), jnp.float32)
    w_down = nrm(ks[25], (DEPTH, D_FF, D_MODEL), jnp.float32) * D_FF ** -0.5
    return {'x_prompt': x_prompt, 'x_sample': x_sample, 'cache_k': cache_k, 'cache_v': cache_v,
            'cache_logf': cache_logf, 'page_table': page_table, 'state_C': state_C, 'state_n': state_n,
            'state_m': state_m, 'state_qkconv': state_qkconv, 'state_ffnconv': state_ffnconv,
            'g_mix': g_mix, 'w_in': w_in, 'b_gate': b_gate, 'w_qkconv': w_qkconv, 'g_qf': g_qf,
            'g_kf': g_kf, 'g_mh': g_mh, 'w_out': w_out, 'g_ffn': g_ffn, 'w_up': w_up,
            'w_ffconv': w_ffconv, 'b_ffconv': b_ffconv, 'w_down': w_down}


def reference(x_prompt, x_sample, cache_k, cache_v, cache_logf, page_table, state_C, state_n, state_m,
              state_qkconv, state_ffnconv, g_mix, w_in, b_gate, w_qkconv, g_qf, g_kf, g_mh, w_out,
              g_ffn, w_up, w_ffconv, b_ffconv, w_down):
    f32 = jnp.float32
    bsz = x_prompt.shape[0]
    dbsz = x_sample.shape[0]
    past_len = page_table.shape[1] * cache_k.shape[2]
    hp = x_prompt.astype(f32)
    hs = x_sample.astype(f32)
    st_prompt = []
    st_sample = []
    for l in range(DEPTH):
        params = (g_mix[l], w_in[l], b_gate[l], w_qkconv[l], g_qf[l], g_kf[l], g_mh[l], w_out[l],
                  g_ffn[l], w_up[l], w_ffconv[l], b_ffconv[l], w_down[l])
        hp, sp = layer_step(
            hp, jnp.zeros((bsz, QK_CONV - 1, 2 * MLSTM_WIDTH), f32), jnp.zeros((bsz, FFN_CONV - 1, D_FF), f32),
            jnp.zeros((bsz, MLSTM_HEADS, MLSTM_HEAD_DIM, MLSTM_HEAD_DIM), f32),
            jnp.zeros((bsz, MLSTM_HEADS, MLSTM_HEAD_DIM), f32), jnp.zeros((bsz, MLSTM_HEADS), f32),
            fox_prompt, *params)
        k_past = cache_k[l][page_table].reshape(dbsz, past_len, FOX_HEADS, FOX_HEAD_DIM)
        v_past = cache_v[l][page_table].reshape(dbsz, past_len, FOX_HEADS, FOX_HEAD_DIM)
        lf_past = cache_logf[l][page_table].reshape(dbsz, past_len, FOX_HEADS)
        fox_fn = functools.partial(fox_sample, k_past=k_past, v_past=v_past, logf_past=lf_past)
        hs, ss = layer_step(hs, state_qkconv[l], state_ffnconv[l], state_C[l], state_n[l], state_m[l],
                            fox_fn, *params)
        st_prompt.append(sp)
        st_sample.append(ss)
    y_prompt = hp.astype(x_prompt.dtype)
    y_sample = hs.astype(x_sample.dtype)
    k_prompt = _stack(st_prompt, 0)
    v_prompt = _stack(st_prompt, 1)
    logf_prompt = _stack(st_prompt, 2)
    C_prompt = _stack(st_prompt, 3)
    n_prompt = _stack(st_prompt, 4)
    m_prompt = _stack(st_prompt, 5)
    qkconv_prompt = _stack(st_prompt, 6)
    ffnconv_prompt = _stack(st_prompt, 7)
    k_sample = _stack(st_sample, 0)
    v_sample = _stack(st_sample, 1)
    logf_sample = _stack(st_sample, 2)
    C_sample = _stack(st_sample, 3)
    n_sample = _stack(st_sample, 4)
    m_sample = _stack(st_sample, 5)
    qkconv_sample = _stack(st_sample, 6)
    ffnconv_sample = _stack(st_sample, 7)
    return (y_prompt, y_sample, k_prompt, v_prompt, logf_prompt, C_prompt, n_prompt, m_prompt,
            qkconv_prompt, ffnconv_prompt, k_sample, v_sample, logf_sample, C_sample, n_sample,
            m_sample, qkconv_sample, ffnconv_sample)
```

```python
import functools

import jax
import jax.numpy as jnp
from jax import lax
from jax.experimental import pallas as pl
from jax.experimental.pallas import tpu as pltpu

F32, BF16 = jnp.float32, jnp.bfloat16

D_MODEL = 1024
MH, MD = 4, 128
MW = MH * MD
FH, FD = 8, 64
FW = FH * FD
QK_CONV, FFN_CONV = 4, 3
D_FF = 2816
N_GATES = 2 * MH + FH
EPS = 1e-6
NEG = -1e30
LANES = 128
SUB = 8
VMEM_LIMIT = 56 * 1024 * 1024

C_QK, C_VM, C_OM = 0, 2 * MW, 3 * MW
C_QF = 4 * MW
C_KF = C_QF + FH * LANES
C_VF = C_KF + FH * LANES
C_OF = C_VF + FH * LANES
C_G = C_OF + FW
NC = C_G + LANES
O_QF = 4 * MW
O_KF, O_VF, O_OF, O_GATE = O_QF + FW, O_QF + 2 * FW, O_QF + 3 * FW, O_QF + 4 * FW
G_FM, G_FF = MH, 2 * MH


def _split3(x):
    a = x.astype(BF16)
    r = x - a.astype(F32)
    b = r.astype(BF16)
    c = (r - b.astype(F32)).astype(BF16)
    return a, b, c


def _dot(a, b):
    return jnp.dot(a, b, preferred_element_type=F32)


def _dot_nt(a, b):
    return lax.dot_general(a, b, (((1,), (1,)), ((), ())), preferred_element_type=F32)


def _dot_tn(a, b):
    return lax.dot_general(a, b, (((0,), (0,)), ((), ())), preferred_element_type=F32)


def _dot3_rhs(m, x):
    a, b, c = _split3(x)
    return _dot(m, c) + _dot(m, b) + _dot(m, a)


def _dot3_lhs(x, m):
    a, b, c = _split3(x)
    return _dot(c, m) + _dot(b, m) + _dot(a, m)


def _log_sigmoid(x):
    return jnp.minimum(x, 0.0) - jnp.log1p(jnp.exp(-jnp.abs(x)))


def _silu(x):
    return x * jax.nn.sigmoid(x)


def _const_spec(shape):
    n = len(shape)
    return pl.BlockSpec(shape, lambda *_: (0,) * n, pipeline_mode=pl.Buffered(1))


def _params(sem):
    return pltpu.CompilerParams(dimension_semantics=sem, vmem_limit_bytes=VMEM_LIMIT)


def _proj_body(x_ref, gmix_ref, w_ref, bg_ref, gq_ref, gk_ref, tri_ref,
               qk_ref, vm_ref, om_ref, qa_ref, ka_ref, kp_ref, va_ref, vp_ref, of_ref, g_ref, fc_ref,
               carry_ref, *, tiles_per_seq):
    tm = x_ref.shape[0]
    x = x_ref[...]
    ms = jnp.mean(x * x, axis=-1, keepdims=True)
    xn = (x * lax.rsqrt(ms + EPS) * gmix_ref[...]).astype(BF16)

    def proj(lo, n):
        return _dot(xn, w_ref[:, lo:lo + n])

    qk_ref[...] = proj(C_QK, 2 * MW)
    vm_ref[...] = proj(C_VM, MW).astype(BF16)
    om_ref[...] = jax.nn.sigmoid(proj(C_OM, MW))
    of_ref[...] = jax.nn.sigmoid(proj(C_OF, FW))

    g = proj(C_G, LANES) + bg_ref[...]
    col = lax.broadcasted_iota(jnp.int32, g.shape, 1)
    gates = jnp.where(col < MH, g, jnp.where(col < N_GATES, _log_sigmoid(g), 0.0))
    g_ref[...] = gates

    cum = _dot3_rhs(tri_ref[...], gates)
    if tiles_per_seq > 1:
        @pl.when(pl.program_id(0) % tiles_per_seq == 0)
        def _():
            carry_ref[...] = jnp.zeros_like(carry_ref)
        cum = cum + carry_ref[...]
        carry_ref[...] = cum[tm - 1:tm, :]
    fc_ref[...] = cum

    nf = -cum
    n1 = nf.astype(BF16).astype(F32)
    r1 = nf - n1
    n2 = r1.astype(BF16).astype(F32)
    n3 = (r1 - n2).astype(BF16).astype(F32)

    pq = proj(C_QF, FH * LANES)
    pk = proj(C_KF, FH * LANES)
    pv = proj(C_VF, FH * LANES)
    lane = lax.broadcasted_iota(jnp.int32, (tm, LANES), 1)
    gq = gq_ref[...]
    gk = gk_ref[...]
    for h in range(FH):
        sl = slice(h * LANES, (h + 1) * LANES)
        xq, xk, xv = pq[:, sl], pk[:, sl], pv[:, sl]
        qn = xq * lax.rsqrt(jnp.sum(xq * xq, axis=-1, keepdims=True) * (1.0 / FD) + EPS) * gq
        kn = xk * lax.rsqrt(jnp.sum(xk * xk, axis=-1, keepdims=True) * (1.0 / FD) + EPS) * gk
        c = G_FF + h
        b1 = jnp.broadcast_to(n1[:, c:c + 1], (tm, LANES))
        b2 = jnp.broadcast_to(n2[:, c:c + 1], (tm, LANES))
        b3 = jnp.broadcast_to(n3[:, c:c + 1], (tm, LANES))
        qa = jnp.where((lane >= FD) & (lane < FD + 3), 1.0, qn * (FD ** -0.5))
        ka = jnp.where(lane == FD, b1, jnp.where(lane == FD + 1, b2, jnp.where(lane == FD + 2, b3, kn)))
        va = jnp.where(lane == FD, 1.0, xv)
        qa_ref[:, sl] = qa.astype(BF16)
        ka_ref[:, sl] = ka.astype(BF16)
        va_ref[:, sl] = va.astype(BF16)
        kp_ref[:, sl] = kn
        vp_ref[:, sl] = xv


def _proj(x2d, gmix, w_pad, bg_pad, gq_pad, gk_pad, tri, *, tm, tiles_per_seq):
    r = x2d.shape[0]
    row = lambda n: pl.BlockSpec((tm, n), lambda i: (i, 0))
    widths = [(2 * MW, F32), (MW, BF16), (MW, F32), (FH * LANES, BF16), (FH * LANES, BF16), (FH * LANES, F32),
              (FH * LANES, BF16), (FH * LANES, F32), (FW, F32), (LANES, F32), (LANES, F32)]
    return pl.pallas_call(
        functools.partial(_proj_body, tiles_per_seq=tiles_per_seq),
        grid=(r // tm,),
        in_specs=[row(D_MODEL), _const_spec((1, D_MODEL)), _const_spec((D_MODEL, NC)), _const_spec((1, LANES)),
                  _const_spec((1, LANES)), _const_spec((1, LANES)), _const_spec((tm, tm))],
        out_specs=[row(n) for n, _ in widths],
        out_shape=[jax.ShapeDtypeStruct((r, n), dt) for n, dt in widths],
        scratch_shapes=[pltpu.VMEM((1, LANES), F32)],
        compiler_params=_params(("arbitrary",)),
        name="proj",
    )(x2d, gmix, w_pad, bg_pad, gq_pad, gk_pad, tri)


def _mlstm_gate_math(h, gates, g_t, cum, cum_t, tot, tot_t, mp_col, cmask, smask):
    i_col, i_row = gates[:, h:h + 1], g_t[h:h + 1, :]
    c = G_FM + h
    b_col, b_row = cum[:, c:c + 1], cum_t[c:c + 1, :]
    t_col, t_row = tot[:, c:c + 1], tot_t[c:c + 1, :]
    logd = jnp.where(cmask, b_col + (i_row - b_row), NEG)
    m_t = jnp.maximum(b_col + mp_col, jnp.max(logd, axis=-1, keepdims=True))
    z = jnp.where(smask, (t_row - b_row) + i_row, NEG)
    m_new = jnp.maximum(t_col + mp_col, jnp.max(z, axis=-1, keepdims=True))
    dmat = jnp.exp(logd - m_t)
    sc = jnp.exp(b_col + mp_col - m_t)
    w_col = jnp.exp(t_col - b_col + i_col - m_new)
    decay = jnp.exp(t_col + mp_col - m_new)
    return dmat, sc, m_t, m_new, w_col, decay


def _mlstm_head_out(q_bf, k_bf, v_bf, dmat, sc, m_t, inter, qn, om, gmh):
    s = _dot_nt(q_bf, k_bf) * dmat
    num = _dot(s.astype(BF16), v_bf) + sc * inter
    den = jnp.sum(s, axis=-1, keepdims=True) + sc * qn
    hc = num / jnp.maximum(jnp.abs(den), jnp.exp(-m_t))
    hn = hc * lax.rsqrt(jnp.mean(hc * hc, axis=-1, keepdims=True) + EPS) * gmh
    return (om * hn).astype(BF16)


def _mlstm_prompt_body(qk_ref, halo_ref, vm_ref, om_ref, g_ref, wc_ref, gmh_ref,
                       out_ref, c_out, n_out, m_out,
                       ext, c_scr, n_scr, m_scr):
    c_idx = pl.program_id(1)
    L = qk_ref.shape[0]

    @pl.when(c_idx == 0)
    def _():
        c_scr[...] = jnp.zeros_like(c_scr)
        n_scr[...] = jnp.zeros_like(n_scr)
        m_scr[...] = jnp.zeros_like(m_scr)

    ext[0:SUB, :] = jnp.where(c_idx == 0, 0.0, halo_ref[...])
    ext[SUB:SUB + L, :] = qk_ref[...]
    conv = wc_ref[0:1, :] * ext[pl.ds(SUB - 3, L), :]
    for j in range(1, QK_CONV):
        conv = conv + wc_ref[j:j + 1, :] * ext[pl.ds(SUB - 3 + j, L), :]
    qk = _silu(conv)

    row = lax.broadcasted_iota(jnp.int32, (L, L), 0)
    colm = lax.broadcasted_iota(jnp.int32, (L, L), 1)
    cmask = colm <= row
    smask = colm >= 0
    tri = jnp.where(cmask, 1.0, 0.0).astype(BF16)
    ones = jnp.ones((L, L), BF16)

    gates = g_ref[...]
    cum = _dot3_rhs(tri, gates)
    tot = _dot3_rhs(ones, gates)
    g_t, cum_t, tot_t = gates.T, cum.T, tot.T

    for h in range(MH):
        sl = slice(h * MD, (h + 1) * MD)
        q = qk[:, sl]
        k = qk[:, MW + h * MD:MW + (h + 1) * MD] * (MD ** -0.5)
        q_bf, k_bf = q.astype(BF16), k.astype(BF16)
        v_bf = vm_ref[:, sl]
        mp = m_scr[h:h + 1, 0:1]
        dmat, sc, m_t, m_new, w_col, decay = _mlstm_gate_math(h, gates, g_t, cum, cum_t, tot, tot_t, mp, cmask, smask)
        c_prev = c_scr[h]
        n_prev = n_scr[h:h + 1, :]
        inter = _dot_nt(q_bf, c_prev.astype(BF16))
        qn = jnp.sum(q * n_prev, axis=-1, keepdims=True)
        out_ref[:, sl] = _mlstm_head_out(q_bf, k_bf, v_bf, dmat, sc, m_t, inter, qn, om_ref[:, sl], gmh_ref[:, sl])
        dec = decay[L - 1:L, :]
        wv = (w_col * v_bf.astype(F32)).astype(BF16)
        c_scr[h] = dec * c_prev + _dot_tn(wv, k_bf)
        n_scr[h:h + 1, :] = dec * n_prev + jnp.sum(w_col * k, axis=0, keepdims=True)
        m_scr[h:h + 1, :] = jnp.broadcast_to(m_new[L - 1:L, :], (1, LANES))

    @pl.when(c_idx == pl.num_programs(1) - 1)
    def _():
        c_out[0] = c_scr[...]
        n_out[0] = n_scr[...]
        m_out[0] = m_scr[...]


def _mlstm_prompt(qk, vm, om, gates, wconv, gmh, *, bsz, seq):
    L = MD
    nck = seq // L
    blk = lambda n: pl.BlockSpec((L, n), lambda b, c: (b * nck + c, 0))
    halo = pl.BlockSpec((SUB, 2 * MW), lambda b, c: (jnp.maximum((b * nck + c) * (L // SUB) - 1, 0), 0))
    st = lambda shape: pl.BlockSpec((1,) + shape, lambda b, c: (b,) + (0,) * len(shape))
    return pl.pallas_call(
        _mlstm_prompt_body,
        grid=(bsz, nck),
        in_specs=[blk(2 * MW), halo, blk(MW), blk(MW), blk(LANES), _const_spec((QK_CONV, 2 * MW)), _const_spec((1, MW))],
        out_specs=[blk(MW), st((MH, MD, MD)), st((SUB, MD)), st((SUB, LANES))],
        out_shape=[jax.ShapeDtypeStruct((bsz * seq, MW), BF16),
                   jax.ShapeDtypeStruct((bsz, MH, MD, MD), F32),
                   jax.ShapeDtypeStruct((bsz, SUB, MD), F32),
                   jax.ShapeDtypeStruct((bsz, SUB, LANES), F32)],
        scratch_shapes=[pltpu.VMEM((SUB + L, 2 * MW), F32), pltpu.VMEM((MH, MD, MD), F32),
                        pltpu.VMEM((SUB, MD), F32), pltpu.VMEM((SUB, LANES), F32)],
        compiler_params=_params(("arbitrary", "arbitrary")),
        name="mlstm_prompt",
    )(qk, qk, vm, om, gates, wconv, gmh)


def _mlstm_sample_body(qk_ref, buf_ref, vm_ref, om_ref, g_ref, mrow_ref, nrow_ref, c_in, n_in, wc_ref, gmh_ref,
                       out_ref, c_out, n_out, mrow_out,
                       q_scr, k_scr, wv_scr, inter_scr, dec_scr, *, t_new):
    L = qk_ref.shape[0]
    nseg = L // t_new
    cur = qk_ref[...].reshape(nseg, t_new, 2 * MW)
    buf = buf_ref[...].reshape(nseg, t_new, 2 * MW)
    tpos = lax.broadcasted_iota(jnp.int32, cur.shape, 1)
    w = lambda j: wc_ref[j:j + 1, :].reshape(1, 1, 2 * MW)
    conv = w(QK_CONV - 1) * cur
    for d in range(1, QK_CONV):
        shifted = jnp.where(tpos >= d, pltpu.roll(cur, d, axis=1), pltpu.roll(buf, d, axis=1))
        conv = conv + w(QK_CONV - 1 - d) * shifted
    qk = _silu(conv).reshape(L, 2 * MW)

    row = lax.broadcasted_iota(jnp.int32, (L, L), 0)
    colm = lax.broadcasted_iota(jnp.int32, (L, L), 1)
    smask = (row // t_new) == (colm // t_new)
    cmask = smask & (colm <= row)
    tri = jnp.where(cmask, 1.0, 0.0).astype(BF16)
    ones = jnp.where(smask, 1.0, 0.0).astype(BF16)
    seg16 = jnp.where(lax.broadcasted_iota(jnp.int32, (nseg, L), 0)
                      == lax.broadcasted_iota(jnp.int32, (nseg, L), 1) // t_new, 1.0, 0.0).astype(BF16)

    gates = g_ref[...]
    cum = _dot3_rhs(tri, gates)
    tot = _dot3_rhs(ones, gates)
    g_t, cum_t, tot_t = gates.T, cum.T, tot.T
    lane = lax.broadcasted_iota(jnp.int32, (L, LANES), 1)

    head = []
    dec_all = jnp.zeros((L, LANES), F32)
    mnew_all = jnp.zeros((L, LANES), F32)
    for h in range(MH):
        sl = slice(h * MD, (h + 1) * MD)
        q = qk[:, sl]
        k = qk[:, MW + h * MD:MW + (h + 1) * MD] * (MD ** -0.5)
        v_bf = vm_ref[:, sl]
        mp = mrow_ref[:, h:h + 1]
        dmat, sc, m_t, m_new, w_col, decay = _mlstm_gate_math(h, gates, g_t, cum, cum_t, tot, tot_t, mp, cmask, smask)
        q_scr[h] = q
        k_scr[h] = k.astype(BF16)
        wv_scr[h] = w_col * v_bf.astype(F32)
        dec_all = jnp.where(lane == h, jnp.broadcast_to(decay, (L, LANES)), dec_all)
        mnew_all = jnp.where(lane == h, jnp.broadcast_to(m_new, (L, LANES)), mnew_all)
        n_add = _dot3_rhs(seg16, w_col * k)
        dec_seg = _dot3_rhs(seg16, decay * (1.0 / t_new) * jnp.ones((L, MD), F32))
        n_out[h] = dec_seg * n_in[h] + n_add
        qn = jnp.sum(q * nrow_ref[:, sl], axis=-1, keepdims=True)
        head.append((dmat, sc, m_t, qn))
    dec_scr[...] = dec_all
    mrow_out[...] = mnew_all

    rowseg = lax.broadcasted_iota(jnp.int32, (L, MD), 0) // t_new

    def seq_body(j, carry):
        r0 = pl.multiple_of(j * t_new, t_new)
        dgrp = dec_scr[pl.ds(r0, t_new), :]
        for h in range(MH):
            c_prev = c_in[j, h]
            q_j = q_scr[h, pl.ds(r0, t_new), :].astype(BF16)
            inter_scr[h, pl.ds(r0, t_new), :] = _dot_nt(q_j, c_prev.astype(BF16))
            wv_j = jnp.where(rowseg == j, wv_scr[h], 0.0).astype(BF16)
            c_out[j, h] = dgrp[0:1, h:h + 1] * c_prev + _dot_tn(wv_j, k_scr[h])
        return carry

    lax.fori_loop(0, nseg, seq_body, 0)

    for h in range(MH):
        sl = slice(h * MD, (h + 1) * MD)
        dmat, sc, m_t, qn = head[h]
        out_ref[:, sl] = _mlstm_head_out(q_scr[h].astype(BF16), k_scr[h], vm_ref[:, sl], dmat, sc, m_t, inter_scr[h], qn,
                                         om_ref[:, sl], gmh_ref[:, sl])


def _mlstm_sample(qk, bufpad, vm, om, gates, mrows, nrows, c_state, n_state_t, wconv, gmh, *, t_new):
    L = MD
    r = qk.shape[0]
    nseg = L // t_new
    blk = lambda n: pl.BlockSpec((L, n), lambda i: (i, 0))
    return pl.pallas_call(
        functools.partial(_mlstm_sample_body, t_new=t_new),
        grid=(r // L,),
        in_specs=[blk(2 * MW), blk(2 * MW), blk(MW), blk(MW), blk(LANES), blk(LANES), blk(MW),
                  pl.BlockSpec((nseg, MH, MD, MD), lambda i: (i, 0, 0, 0)),
                  pl.BlockSpec((MH, nseg, MD), lambda i: (0, i, 0)),
                  _const_spec((QK_CONV, 2 * MW)), _const_spec((1, MW))],
        out_specs=[blk(MW), pl.BlockSpec((nseg, MH, MD, MD), lambda i: (i, 0, 0, 0)),
                   pl.BlockSpec((MH, nseg, MD), lambda i: (0, i, 0)), blk(LANES)],
        out_shape=[jax.ShapeDtypeStruct((r, MW), BF16), jax.ShapeDtypeStruct(c_state.shape, F32),
                   jax.ShapeDtypeStruct(n_state_t.shape, F32), jax.ShapeDtypeStruct((r, LANES), F32)],
        scratch_shapes=[pltpu.VMEM((MH, L, MD), F32), pltpu.VMEM((MH, L, MD), BF16), pltpu.VMEM((MH, L, MD), F32),
                        pltpu.VMEM((MH, L, MD), F32), pltpu.VMEM((L, LANES), F32)],
        compiler_params=_params(("arbitrary",)),
        name="mlstm_sample",
    )(qk, bufpad, vm, om, gates, mrows, nrows, c_state, n_state_t, wconv, gmh)


def _fox_prompt_body(qa_ref, ka_ref, va_ref, o_ref, m_scr, acc_scr, *, tq):
    i = pl.program_id(2)
    row = lax.broadcasted_iota(jnp.int32, (tq, tq), 0)
    colm = lax.broadcasted_iota(jnp.int32, (tq, tq), 1)
    outs = []
    for hh in range(2):
        sl = slice(hh * LANES, (hh + 1) * LANES)
        q = qa_ref[:, sl]
        m_scr[...] = jnp.full(m_scr.shape, NEG, F32)
        acc_scr[...] = jnp.zeros_like(acc_scr)

        def block(j, masked):
            r0 = pl.multiple_of(j * tq, tq)
            s = _dot_nt(q, ka_ref[pl.ds(r0, tq), sl])
            if masked:
                s = jnp.where(colm <= row, s, NEG)
            m_old = m_scr[:, 0:1]
            m_new = jnp.maximum(m_old, jnp.max(s, axis=-1, keepdims=True))
            p = jnp.exp(s - m_new)
            acc_scr[...] = jnp.exp(m_old - m_new) * acc_scr[...] + _dot(p.astype(BF16), va_ref[pl.ds(r0, tq), sl])
            m_scr[...] = jnp.broadcast_to(m_new, m_scr.shape)

        def loop_body(j, carry):
            block(j, False)
            return carry

        lax.fori_loop(0, i, loop_body, 0)
        block(i, True)
        acc = acc_scr[...]
        outs.append(acc[:, 0:FD] / acc[:, FD:FD + 1])
    o_ref[...] = jnp.concatenate(outs, axis=1)


def _fox_prompt(qa, ka, va, *, bsz, seq, tq):
    nq = seq // tq
    w2 = 2 * LANES
    return pl.pallas_call(
        functools.partial(_fox_prompt_body, tq=tq),
        grid=(bsz, FH // 2, nq),
        in_specs=[pl.BlockSpec((tq, w2), lambda b, hp, i: (b * nq + i, hp)),
                  pl.BlockSpec((seq, w2), lambda b, hp, i: (b, hp)),
                  pl.BlockSpec((seq, w2), lambda b, hp, i: (b, hp))],
        out_specs=pl.BlockSpec((tq, 2 * FD), lambda b, hp, i: (b * nq + i, hp)),
        out_shape=jax.ShapeDtypeStruct((bsz * seq, FW), F32),
        scratch_shapes=[pltpu.VMEM((tq, LANES), F32), pltpu.VMEM((tq, LANES), F32)],
        compiler_params=_params(("arbitrary", "arbitrary", "arbitrary")),
        name="fox_prompt",
    )(qa, ka, va)


def _fox_sample_body(pt_ref, q_ref, kn_ref, vn_ref, cn_ref, rep_ref, msuf_ref, msel_ref, lf_hbm, k_hbm, v_hbm,
                     o_ref,
                     kbuf, vbuf, lfbuf, bias_scr, m_scr, l_scr, acc_scr, ksem, vsem, lfsem,
                     *, nseq, npg, ch, t_new):
    b = pl.program_id(0)
    nch = npg // ch
    nkey = ch * LANES
    nrow = FH * t_new

    def kv_copies(seq, c, slot):
        cps = []
        for pi in range(ch):
            pg = pt_ref[seq, c * ch + pi]
            dst = pl.ds(pi * LANES, LANES)
            cps.append(pltpu.make_async_copy(k_hbm.at[pg], kbuf.at[slot, :, :, dst], ksem.at[slot]))
            cps.append(pltpu.make_async_copy(v_hbm.at[pg], vbuf.at[slot, :, :, dst], vsem.at[slot]))
        return cps

    def lf_copy(seq, p, slot):
        return pltpu.make_async_copy(lf_hbm.at[pt_ref[seq, p]], lfbuf.at[slot, p], lfsem.at[slot])

    def lf_start(seq, slot):
        def body(p, carry):
            lf_copy(seq, p, slot).start()
            return carry
        lax.fori_loop(0, npg, body, 0)

    def lf_wait(seq, slot):
        def body(p, carry):
            lf_copy(seq, p, slot).wait()
            return carry
        lax.fori_loop(0, npg, body, 0)

    @pl.when(b == 0)
    def _():
        lf_start(0, 0)
        for cp in kv_copies(0, 0, 0):
            cp.start()

    lslot = b % 2
    lf_wait(b, lslot)

    @pl.when(b + 1 < nseq)
    def _():
        lf_start(b + 1, 1 - lslot)

    lf2 = lfbuf[lslot].reshape(npg * FH, LANES)
    suf_in = _dot3_lhs(lf2, msuf_ref[...])
    cross = jnp.sum(_dot3_rhs(msel_ref[...], lf2), axis=-1, keepdims=True)
    bias_scr[...] = (suf_in + cross).reshape(npg, FH, LANES)

    q = q_ref[0].astype(F32)
    rr = lax.broadcasted_iota(jnp.int32, (nrow, FW), 0) // t_new
    cc = lax.broadcasted_iota(jnp.int32, (nrow, FW), 1) // FD
    qbd = jnp.where(rr == cc, jnp.concatenate([q] * FH, axis=0), 0.0).astype(BF16)
    rep = rep_ref[...]

    def bias_rows(bias8):
        n = bias8.shape[1]
        a, b2, c2 = _split3(bias8)
        return jnp.concatenate([a.astype(F32), b2.astype(F32), c2.astype(F32),
                                jnp.zeros((LANES - 3 * FH, n), F32)], axis=0).astype(BF16)

    pad = jnp.zeros((LANES - t_new, FW), F32)
    kn = jnp.concatenate([kn_ref[0], pad], axis=0).astype(BF16)
    vn = jnp.concatenate([vn_ref[0], pad], axis=0).astype(BF16)
    s = _dot_nt(qbd, kn) + _dot(rep, bias_rows(-cn_ref[0]))
    qpos = lax.broadcasted_iota(jnp.int32, (nrow, LANES), 0) % t_new
    kpos = lax.broadcasted_iota(jnp.int32, (nrow, LANES), 1)
    s = jnp.where(kpos <= qpos, s, NEG)
    m0 = jnp.max(s, axis=-1, keepdims=True)
    p = jnp.exp(s - m0)
    m_scr[...] = jnp.broadcast_to(m0, m_scr.shape)
    l_scr[...] = jnp.broadcast_to(jnp.sum(p, axis=-1, keepdims=True), l_scr.shape)
    acc_scr[...] = _dot(p.astype(BF16), vn)

    def chunk_body(c, carry):
        g = b * nch + c
        slot = g % 2
        for cp in kv_copies(b, c, slot):
            cp.wait()

        @pl.when(g + 1 < nseq * nch)
        def _():
            wrap = c + 1 == nch
            nb = jnp.where(wrap, b + 1, b)
            ncx = jnp.where(wrap, 0, c + 1)
            for cp in kv_copies(nb, ncx, 1 - slot):
                cp.start()

        kt = kbuf[slot].reshape(FW, nkey).astype(BF16)
        blk = bias_scr[pl.ds(c * ch, ch)]
        bias8 = jnp.concatenate([blk[pi] for pi in range(ch)], axis=1)
        s = _dot(qbd, kt) + _dot(rep, bias_rows(bias8))
        m_old = m_scr[:, 0:1]
        m_new = jnp.maximum(m_old, jnp.max(s, axis=-1, keepdims=True))
        p = jnp.exp(s - m_new)
        alpha = jnp.exp(m_old - m_new)
        vt = vbuf[slot].reshape(FW, nkey).astype(BF16)
        acc_scr[...] = alpha * acc_scr[...] + _dot_nt(p.astype(BF16), vt)
        l_scr[...] = alpha * l_scr[...] + jnp.sum(p, axis=-1, keepdims=True)
        m_scr[...] = jnp.broadcast_to(m_new, m_scr.shape)
        return carry

    lax.fori_loop(0, nch, chunk_body, 0)

    accn = acc_scr[...] / l_scr[:, 0:1]
    out = jnp.zeros((t_new, FW), F32)
    hcol = lax.broadcasted_iota(jnp.int32, (t_new, FW), 1) // FD
    for h in range(FH):
        out = jnp.where(hcol == h, accn[h * t_new:(h + 1) * t_new, :], out)
    o_ref[0] = out


def _fox_sample(page_table, q_s, k_new, v_new, c_new, rep, msuf, msel, lf_t, k_t, v_t, *, ch):
    nseq, t_new, _ = q_s.shape
    npg = page_table.shape[1]
    seqblk = lambda n: pl.BlockSpec((1, t_new, n), lambda b, pt: (b, 0, 0))
    cst = lambda shape: pl.BlockSpec(shape, lambda b, pt: (0,) * len(shape), pipeline_mode=pl.Buffered(1))
    anyspec = pl.BlockSpec(memory_space=pl.ANY)
    nrow = FH * t_new
    grid_spec = pltpu.PrefetchScalarGridSpec(
        num_scalar_prefetch=1,
        grid=(nseq,),
        in_specs=[seqblk(FW), seqblk(FW), seqblk(FW), pl.BlockSpec((1, FH, LANES), lambda b, pt: (b, 0, 0)),
                  cst((nrow, LANES)), cst((LANES, LANES)), cst((npg * FH, npg * FH)), anyspec, anyspec, anyspec],
        out_specs=seqblk(FW),
        scratch_shapes=[pltpu.VMEM((2, FH, FD, ch * LANES), F32), pltpu.VMEM((2, FH, FD, ch * LANES), F32),
                        pltpu.VMEM((2, npg, FH, LANES), F32), pltpu.VMEM((npg, FH, LANES), F32),
                        pltpu.VMEM((nrow, LANES), F32), pltpu.VMEM((nrow, LANES), F32), pltpu.VMEM((nrow, FW), F32),
                        pltpu.SemaphoreType.DMA((2,)), pltpu.SemaphoreType.DMA((2,)), pltpu.SemaphoreType.DMA((2,))],
    )
    return pl.pallas_call(
        functools.partial(_fox_sample_body, nseq=nseq, npg=npg, ch=ch, t_new=t_new),
        grid_spec=grid_spec,
        out_shape=jax.ShapeDtypeStruct((nseq, t_new, FW), F32),
        compiler_params=_params(("arbitrary",)),
        name="fox_sample",
    )(page_table, q_s, k_new, v_new, c_new, rep, msuf, msel, lf_t, k_t, v_t)


def _ffn_body(x_ref, om_ref, af_ref, of_ref, buf_ref, wo_ref, gf_ref, wu_ref, wc_ref, bc_ref, wd_ref,
              y_ref, tail_ref, ext, *, tiles_per_seq, t_new):
    tm = x_ref.shape[0]
    mixin = jnp.concatenate([om_ref[...], (of_ref[...] * af_ref[...]).astype(BF16)], axis=1)
    h = x_ref[...] + _dot(mixin, wo_ref[...])
    hn = (h * lax.rsqrt(jnp.mean(h * h, axis=-1, keepdims=True) + EPS) * gf_ref[...]).astype(BF16)
    up_a = _dot(hn, wu_ref[:, 0:D_FF])
    up_g = _dot(hn, wu_ref[:, D_FF:2 * D_FF])
    if t_new is None:
        @pl.when(pl.program_id(0) % tiles_per_seq == 0)
        def _():
            ext[0:SUB, :] = jnp.zeros((SUB, D_FF), F32)
        ext[SUB:SUB + tm, :] = up_a
        conv = wc_ref[0:1, :] * ext[pl.ds(SUB - 2, tm), :]
        for j in range(1, FFN_CONV):
            conv = conv + wc_ref[j:j + 1, :] * ext[pl.ds(SUB - 2 + j, tm), :]
        ext[0:SUB, :] = up_a[tm - SUB:tm, :]
        tail_ref[...] = up_a[tm - SUB:tm, :]
    else:
        nseg = tm // t_new
        cur = up_a.reshape(nseg, t_new, D_FF)
        buf = buf_ref[...].reshape(nseg, t_new, D_FF)
        tpos = lax.broadcasted_iota(jnp.int32, cur.shape, 1)
        w = lambda j: wc_ref[j:j + 1, :].reshape(1, 1, D_FF)
        conv = w(FFN_CONV - 1) * cur
        for d in range(1, FFN_CONV):
            shifted = jnp.where(tpos >= d, pltpu.roll(cur, d, axis=1), pltpu.roll(buf, d, axis=1))
            conv = conv + w(FFN_CONV - 1 - d) * shifted
        conv = conv.reshape(tm, D_FF)
        tail_ref[...] = up_a
    act = (_silu(conv + bc_ref[...]) * up_g).astype(BF16)
    y_ref[...] = h + _dot(act, wd_ref[...])


def _ffn(x2d, outm, af, of, bufpad, w_out, g_ffn, w_up, w_ffconv, b_ffconv, w_down, *, tm, tiles_per_seq, t_new):
    r = x2d.shape[0]
    row = lambda n: pl.BlockSpec((tm, n), lambda i: (i, 0))
    if t_new is None:
        tail_spec = pl.BlockSpec((SUB, D_FF), lambda i: (i, 0))
        tail_shape = jax.ShapeDtypeStruct((r // tm * SUB, D_FF), F32)
        buf_spec = pl.BlockSpec((SUB, D_FF), lambda i: (0, 0))
    else:
        tail_spec = row(D_FF)
        tail_shape = jax.ShapeDtypeStruct((r, D_FF), F32)
        buf_spec = row(D_FF)
    return pl.pallas_call(
        functools.partial(_ffn_body, tiles_per_seq=tiles_per_seq, t_new=t_new),
        grid=(r // tm,),
        in_specs=[row(D_MODEL), row(MW), row(FW), row(FW), buf_spec,
                  _const_spec((D_MODEL, D_MODEL)), _const_spec((1, D_MODEL)), _const_spec((D_MODEL, 2 * D_FF)),
                  _const_spec((FFN_CONV, D_FF)), _const_spec((1, D_FF)), _const_spec((D_FF, D_MODEL))],
        out_specs=[row(D_MODEL), tail_spec],
        out_shape=[jax.ShapeDtypeStruct((r, D_MODEL), F32), tail_shape],
        scratch_shapes=[pltpu.VMEM((SUB + tm, D_FF), F32)],
        compiler_params=_params(("arbitrary",)),
        name="ffn",
    )(x2d, outm, af, of, bufpad, w_out, g_ffn, w_up, w_ffconv, b_ffconv, w_down)


def _head_slots(w):
    d = w.shape[0]
    return jnp.pad(w.reshape(d, FH, FD), ((0, 0), (0, 0), (0, LANES - FD))).reshape(d, FH * LANES)


def _lane_pad(v, n=LANES):
    return jnp.pad(v, ((0, 0), (0, n - v.shape[1])))


def _state_rows(state, t_new):
    nseq, k1, c = state.shape
    return jnp.pad(state, ((0, 0), (t_new - k1, 0), (0, 0))).reshape(nseq * t_new, c)


def _unslot(a, lead):
    return a.reshape(lead + (FH, LANES))[..., :FD]


def kernel(x_prompt, x_sample, cache_k, cache_v, cache_logf, page_table, state_C, state_n, state_m, state_qkconv,
           state_ffnconv, g_mix, w_in, b_gate, w_qkconv, g_qf, g_kf, g_mh, w_out, g_ffn, w_up, w_ffconv, b_ffconv,
           w_down):
    assert w_in.shape[0] == 1, "single-layer kernel"
    bsz, seq, _ = x_prompt.shape
    dbsz, t_new, _ = x_sample.shape
    tm = 256

    w = w_in[0]
    w_pad = jnp.concatenate(
        [w[:, 0:O_QF], _head_slots(w[:, O_QF:O_KF]), _head_slots(w[:, O_KF:O_VF]), _head_slots(w[:, O_VF:O_OF]),
         w[:, O_OF:O_GATE], _lane_pad(w[:, O_GATE:])], axis=1).astype(BF16)
    bg_pad = _lane_pad(b_gate.astype(F32))
    gq_pad = _lane_pad(g_qf.astype(F32))
    gk_pad = _lane_pad(g_kf.astype(F32))
    gmix = g_mix.astype(F32)
    wconv = w_qkconv[0].astype(F32)
    gmh = g_mh.astype(F32)
    wo_b, wu_b, wd_b = w_out[0].astype(BF16), w_up[0].astype(BF16), w_down[0].astype(BF16)
    gffn, wfc, bfc = g_ffn.astype(F32), w_ffconv[0].astype(F32), b_ffconv.astype(F32)

    ridx = jnp.arange(tm)
    tri_p = (ridx[None, :] <= ridx[:, None]).astype(BF16)
    tri_s = ((ridx[None, :] <= ridx[:, None]) & (ridx[None, :] // t_new == ridx[:, None] // t_new)).astype(BF16)

    xp = x_prompt.reshape(bsz * seq, D_MODEL).astype(F32)
    (qk_p, vm_p, om_p, qa_p, ka_p, kp_p, va_p, vp_p, of_p, g_p, _) = _proj(
        xp, gmix, w_pad, bg_pad, gq_pad, gk_pad, tri_p, tm=tm, tiles_per_seq=seq // tm)
    outm_p, c_p, n_p, m_p = _mlstm_prompt(qk_p, vm_p, om_p, g_p, wconv, gmh, bsz=bsz, seq=seq)
    af_p = _fox_prompt(qa_p, ka_p, va_p, bsz=bsz, seq=seq, tq=512)
    y_p, tail_p = _ffn(xp, outm_p, af_p, of_p, jnp.zeros((SUB, D_FF), F32), wo_b, gffn, wu_b, wfc, bfc, wd_b,
                       tm=tm, tiles_per_seq=seq // tm, t_new=None)

    xs = x_sample.reshape(dbsz * t_new, D_MODEL).astype(F32)
    (qk_s, vm_s, om_s, qa_s, _, kp_s, _, vp_s, of_s, g_s, fc_s) = _proj(
        xs, gmix, w_pad, bg_pad, gq_pad, gk_pad, tri_s, tm=tm, tiles_per_seq=1)
    mrows = _lane_pad(jnp.repeat(state_m[0].astype(F32), t_new, axis=0))
    nrows = jnp.repeat(state_n[0].astype(F32).reshape(dbsz, MW), t_new, axis=0)
    n_t = jnp.transpose(state_n[0].astype(F32), (1, 0, 2))
    outm_s, c_s, n_s_t, mrow_s = _mlstm_sample(
        qk_s, _state_rows(state_qkconv[0].astype(F32), t_new), vm_s, om_s, g_s, mrows, nrows,
        state_C[0].astype(F32), n_t, wconv, gmh, t_new=t_new)

    k_new = _unslot(kp_s, (dbsz, t_new))
    v_new = _unslot(vp_s, (dbsz, t_new))
    q_s = _unslot(qa_s, (dbsz, t_new)).reshape(dbsz, t_new, FW)
    c_new = _lane_pad(jnp.transpose(fc_s.reshape(dbsz, t_new, LANES)[:, :, G_FF:G_FF + FH], (0, 2, 1))
                      .reshape(dbsz * FH, t_new)).reshape(dbsz, FH, LANES)
    npg = page_table.shape[1]
    nrow = FH * t_new
    rr = jnp.arange(nrow)[:, None] // t_new
    cc = jnp.arange(LANES)[None, :]
    rep = ((cc < 3 * FH) & (cc % FH == rr)).astype(BF16)
    kk = jnp.arange(LANES)
    msuf = (kk[:, None] > kk[None, :]).astype(BF16)
    pr = jnp.arange(npg * FH)
    msel = ((pr[None, :] % FH == pr[:, None] % FH) & (pr[None, :] // FH > pr[:, None] // FH)).astype(BF16)
    k_t = jnp.transpose(cache_k[0], (0, 2, 3, 1)).astype(F32)
    v_t = jnp.transpose(cache_v[0], (0, 2, 3, 1)).astype(F32)
    lf_t = jnp.transpose(cache_logf[0], (0, 2, 1)).astype(F32)
    af_s = _fox_sample(page_table.astype(jnp.int32), q_s, k_new.reshape(dbsz, t_new, FW),
                       v_new.reshape(dbsz, t_new, FW), c_new, rep, msuf, msel, lf_t, k_t, v_t, ch=8)
    y_s, tail_s = _ffn(xs, outm_s, af_s.reshape(dbsz * t_new, FW), of_s,
                       _state_rows(state_ffnconv[0].astype(F32), t_new), wo_b, gffn, wu_b, wfc, bfc, wd_b,
                       tm=tm, tiles_per_seq=1, t_new=t_new)

    dt = x_prompt.dtype
    y_prompt = y_p.reshape(bsz, seq, D_MODEL).astype(dt)
    y_sample = y_s.reshape(dbsz, t_new, D_MODEL).astype(x_sample.dtype)
    k_prompt = _unslot(kp_p, (bsz, seq))[None]
    v_prompt = _unslot(vp_p, (bsz, seq))[None]
    logf_prompt = g_p.reshape(bsz, seq, LANES)[:, :, G_FF:G_FF + FH][None]
    c_prompt = c_p[None]
    n_prompt = n_p[:, :MH, :][None]
    m_prompt = m_p[:, :MH, 0][None]
    qkconv_prompt = qk_p.reshape(bsz, seq, 2 * MW)[:, seq - (QK_CONV - 1):, :][None]
    ffnconv_prompt = tail_p.reshape(bsz, seq // tm, SUB, D_FF)[:, -1, SUB - (FFN_CONV - 1):, :][None]
    k_sample = k_new[None]
    v_sample = v_new[None]
    logf_sample = g_s.reshape(dbsz, t_new, LANES)[:, :, G_FF:G_FF + FH][None]
    c_sample = c_s[None]
    n_sample = jnp.transpose(n_s_t, (1, 0, 2))[None]
    m_sample = mrow_s.reshape(dbsz, t_new, LANES)[:, t_new - 1, :MH][None]
    qkconv_sample = qk_s.reshape(dbsz, t_new, 2 * MW)[:, t_new - (QK_CONV - 1):, :][None]
    ffnconv_sample = tail_s.reshape(dbsz, t_new, D_FF)[:, t_new - (FFN_CONV - 1):, :][None]
    return (y_prompt, y_sample, k_prompt, v_prompt, logf_prompt, c_prompt, n_prompt, m_prompt, qkconv_prompt,
            ffnconv_prompt, k_sample, v_sample, logf_sample, c_sample, n_sample, m_sample, qkconv_sample,
            ffnconv_sample)
```

```python
import functools

import jax
import jax.numpy as jnp
from jax import lax
from jax.experimental import pallas as pl
from jax.experimental.pallas import tpu as pltpu

F32, BF16 = jnp.float32, jnp.bfloat16

D_MODEL = 1024
MH, MD = 4, 128
MW = MH * MD
FH, FD = 8, 64
FW = FH * FD
QK_CONV, FFN_CONV = 4, 3
D_FF = 2816
N_GATES = 2 * MH + FH
EPS = 1e-6
NEG = -1e30
LANES = 128
SUB = 8
VMEM_LIMIT = 56 * 1024 * 1024

C_QK, C_VM, C_OM = 0, 2 * MW, 3 * MW
C_QF = 4 * MW
C_KF = C_QF + FH * LANES
C_VF = C_KF + FH * LANES
C_OF = C_VF + FH * LANES
C_G = C_OF + FW
NC = C_G + LANES
O_QF = 4 * MW
O_KF, O_VF, O_OF, O_GATE = O_QF + FW, O_QF + 2 * FW, O_QF + 3 * FW, O_QF + 4 * FW
G_FM, G_FF = MH, 2 * MH


def _split3(x):
    a = x.astype(BF16)
    r = x - a.astype(F32)
    b = r.astype(BF16)
    c = (r - b.astype(F32)).astype(BF16)
    return a, b, c


def _dot(a, b):
    return jnp.dot(a, b, preferred_element_type=F32)


def _dot_nt(a, b):
    return lax.dot_general(a, b, (((1,), (1,)), ((), ())), preferred_element_type=F32)


def _dot_tn(a, b):
    return lax.dot_general(a, b, (((0,), (0,)), ((), ())), preferred_element_type=F32)


def _dot3_rhs(m, x):
    a, b, c = _split3(x)
    return _dot(m, c) + _dot(m, b) + _dot(m, a)


def _dot3_lhs(x, m):
    a, b, c = _split3(x)
    return _dot(c, m) + _dot(b, m) + _dot(a, m)


def _log_sigmoid(x):
    return jnp.minimum(x, 0.0) - jnp.log1p(jnp.exp(-jnp.abs(x)))


def _silu(x):
    return x * jax.nn.sigmoid(x)


def _const_spec(shape):
    n = len(shape)
    return pl.BlockSpec(shape, lambda *_: (0,) * n, pipeline_mode=pl.Buffered(1))


def _params(sem):
    return pltpu.CompilerParams(dimension_semantics=sem, vmem_limit_bytes=VMEM_LIMIT)


def _proj_body(x_ref, gmix_ref, w_ref, bg_ref, gq_ref, gk_ref, tri_ref,
               qk_ref, vm_ref, om_ref, qa_ref, ka_ref, kp_ref, va_ref, vp_ref, of_ref, g_ref, fc_ref,
               carry_ref, *, tiles_per_seq):
    tm = x_ref.shape[0]
    x = x_ref[...]
    ms = jnp.mean(x * x, axis=-1, keepdims=True)
    xn = (x * lax.rsqrt(ms + EPS) * gmix_ref[...]).astype(BF16)

    def proj(lo, n):
        return _dot(xn, w_ref[:, lo:lo + n])

    qk_ref[...] = proj(C_QK, 2 * MW)
    vm_ref[...] = proj(C_VM, MW).astype(BF16)
    om_ref[...] = jax.nn.sigmoid(proj(C_OM, MW))
    of_ref[...] = jax.nn.sigmoid(proj(C_OF, FW))

    g = proj(C_G, LANES) + bg_ref[...]
    col = lax.broadcasted_iota(jnp.int32, g.shape, 1)
    gates = jnp.where(col < MH, g, jnp.where(col < N_GATES, _log_sigmoid(g), 0.0))
    g_ref[...] = gates

    cum = _dot3_rhs(tri_ref[...], gates)
    if tiles_per_seq > 1:
        @pl.when(pl.program_id(0) % tiles_per_seq == 0)
        def _():
            carry_ref[...] = jnp.zeros_like(carry_ref)
        cum = cum + carry_ref[...]
        carry_ref[...] = cum[tm - 1:tm, :]
    fc_ref[...] = cum

    nf = -cum
    n1 = nf.astype(BF16).astype(F32)
    r1 = nf - n1
    n2 = r1.astype(BF16).astype(F32)
    n3 = (r1 - n2).astype(BF16).astype(F32)

    pq = proj(C_QF, FH * LANES)
    pk = proj(C_KF, FH * LANES)
    pv = proj(C_VF, FH * LANES)
    lane = lax.broadcasted_iota(jnp.int32, (tm, LANES), 1)
    gq = gq_ref[...]
    gk = gk_ref[...]
    for h in range(FH):
        sl = slice(h * LANES, (h + 1) * LANES)
        xq, xk, xv = pq[:, sl], pk[:, sl], pv[:, sl]
        qn = xq * lax.rsqrt(jnp.sum(xq * xq, axis=-1, keepdims=True) * (1.0 / FD) + EPS) * gq
        kn = xk * lax.rsqrt(jnp.sum(xk * xk, axis=-1, keepdims=True) * (1.0 / FD) + EPS) * gk
        c = G_FF + h
        b1 = jnp.broadcast_to(n1[:, c:c + 1], (tm, LANES))
        b2 = jnp.broadcast_to(n2[:, c:c + 1], (tm, LANES))
        b3 = jnp.broadcast_to(n3[:, c:c + 1], (tm, LANES))
        qa = jnp.where((lane >= FD) & (lane < FD + 3), 1.0, qn * (FD ** -0.5))
        ka = jnp.where(lane == FD, b1, jnp.where(lane == FD + 1, b2, jnp.where(lane == FD + 2, b3, kn)))
        va = jnp.where(lane == FD, 1.0, xv)
        qa_ref[:, sl] = qa.astype(BF16)
        ka_ref[:, sl] = ka.astype(BF16)
        va_ref[:, sl] = va.astype(BF16)
        kp_ref[:, sl] = kn
        vp_ref[:, sl] = xv


def _proj(x2d, gmix, w_pad, bg_pad, gq_pad, gk_pad, tri, *, tm, tiles_per_seq):
    r = x2d.shape[0]
    row = lambda n: pl.BlockSpec((tm, n), lambda i: (i, 0))
    widths = [(2 * MW, F32), (MW, BF16), (MW, F32), (FH * LANES, BF16), (FH * LANES, BF16), (FH * LANES, F32),
              (FH * LANES, BF16), (FH * LANES, F32), (FW, F32), (LANES, F32), (LANES, F32)]
    return pl.pallas_call(
        functools.partial(_proj_body, tiles_per_seq=tiles_per_seq),
        grid=(r // tm,),
        in_specs=[row(D_MODEL), _const_spec((1, D_MODEL)), _const_spec((D_MODEL, NC)), _const_spec((1, LANES)),
                  _const_spec((1, LANES)), _const_spec((1, LANES)), _const_spec((tm, tm))],
        out_specs=[row(n) for n, _ in widths],
        out_shape=[jax.ShapeDtypeStruct((r, n), dt) for n, dt in widths],
        scratch_shapes=[pltpu.VMEM((1, LANES), F32)],
        compiler_params=_params(("arbitrary",)),
        name="proj",
    )(x2d, gmix, w_pad, bg_pad, gq_pad, gk_pad, tri)


def _mlstm_gate_math(h, gates, g_t, cum, cum_t, tot, tot_t, mp_col, cmask, smask):
    i_col, i_row = gates[:, h:h + 1], g_t[h:h + 1, :]
    c = G_FM + h
    b_col, b_row = cum[:, c:c + 1], cum_t[c:c + 1, :]
    t_col, t_row = tot[:, c:c + 1], tot_t[c:c + 1, :]
    logd = jnp.where(cmask, b_col + (i_row - b_row), NEG)
    m_t = jnp.maximum(b_col + mp_col, jnp.max(logd, axis=-1, keepdims=True))
    z = jnp.where(smask, (t_row - b_row) + i_row, NEG)
    m_new = jnp.maximum(t_col + mp_col, jnp.max(z, axis=-1, keepdims=True))
    dmat = jnp.exp(logd - m_t)
    sc = jnp.exp(b_col + mp_col - m_t)
    w_col = jnp.exp(t_col - b_col + i_col - m_new)
    decay = jnp.exp(t_col + mp_col - m_new)
    return dmat, sc, m_t, m_new, w_col, decay


def _mlstm_head_out(q_bf, k_bf, v_bf, dmat, sc, m_t, inter, qn, om, gmh):
    s = _dot_nt(q_bf, k_bf) * dmat
    num = _dot(s.astype(BF16), v_bf) + sc * inter
    den = jnp.sum(s, axis=-1, keepdims=True) + sc * qn
    hc = num / jnp.maximum(jnp.abs(den), jnp.exp(-m_t))
    hn = hc * lax.rsqrt(jnp.mean(hc * hc, axis=-1, keepdims=True) + EPS) * gmh
    return (om * hn).astype(BF16)


def _mlstm_prompt_body(qk_ref, halo_ref, vm_ref, om_ref, g_ref, wc_ref, gmh_ref,
                       out_ref, c_out, n_out, m_out,
                       ext, c_scr, n_scr, m_scr):
    c_idx = pl.program_id(0)
    nb, L = qk_ref.shape[0], qk_ref.shape[1]

    @pl.when(c_idx == 0)
    def _():
        c_scr[...] = jnp.zeros_like(c_scr)
        n_scr[...] = jnp.zeros_like(n_scr)
        m_scr[...] = jnp.zeros_like(m_scr)

    row = lax.broadcasted_iota(jnp.int32, (L, L), 0)
    colm = lax.broadcasted_iota(jnp.int32, (L, L), 1)
    cmask = colm <= row
    smask = colm >= 0
    tri = jnp.where(cmask, 1.0, 0.0).astype(BF16)
    ones = jnp.ones((L, L), BF16)

    for b in range(nb):
        ext[b, 0:SUB, :] = jnp.where(c_idx == 0, 0.0, halo_ref[b])
        ext[b, SUB:SUB + L, :] = qk_ref[b]
        conv = wc_ref[0:1, :] * ext[b, pl.ds(SUB - 3, L), :]
        for j in range(1, QK_CONV):
            conv = conv + wc_ref[j:j + 1, :] * ext[b, pl.ds(SUB - 3 + j, L), :]
        qk = _silu(conv)

        gates = g_ref[b]
        cum = _dot3_rhs(tri, gates)
        tot = _dot3_rhs(ones, gates)
        g_t, cum_t, tot_t = gates.T, cum.T, tot.T

        for h in range(MH):
            sl = slice(h * MD, (h + 1) * MD)
            q = qk[:, sl]
            k = qk[:, MW + h * MD:MW + (h + 1) * MD] * (MD ** -0.5)
            q_bf, k_bf = q.astype(BF16), k.astype(BF16)
            v_bf = vm_ref[b, :, sl]
            mp = m_scr[b, h:h + 1, 0:1]
            dmat, sc, m_t, m_new, w_col, decay = _mlstm_gate_math(h, gates, g_t, cum, cum_t, tot, tot_t, mp, cmask, smask)
            c_prev = c_scr[b, h]
            n_prev = n_scr[b, h:h + 1, :]
            inter = _dot_nt(q_bf, c_prev.astype(BF16))
            qn = jnp.sum(q * n_prev, axis=-1, keepdims=True)
            out_ref[b, :, sl] = _mlstm_head_out(q_bf, k_bf, v_bf, dmat, sc, m_t, inter, qn, om_ref[b, :, sl],
                                                gmh_ref[:, sl])
            dec = decay[L - 1:L, :]
            wv = (w_col * v_bf.astype(F32)).astype(BF16)
            c_scr[b, h] = dec * c_prev + _dot_tn(wv, k_bf)
            n_scr[b, h:h + 1, :] = dec * n_prev + jnp.sum(w_col * k, axis=0, keepdims=True)
            m_scr[b, h:h + 1, :] = jnp.broadcast_to(m_new[L - 1:L, :], (1, LANES))

    @pl.when(c_idx == pl.num_programs(0) - 1)
    def _():
        c_out[...] = c_scr[...]
        n_out[...] = n_scr[...]
        m_out[...] = m_scr[...]


def _mlstm_prompt(qk, vm, om, gates, wconv, gmh, *, bsz, seq):
    L = MD
    nck = seq // L
    as3 = lambda a: a.reshape(bsz, seq, a.shape[-1])
    blk = lambda n: pl.BlockSpec((bsz, L, n), lambda c: (0, c, 0))
    halo = pl.BlockSpec((bsz, SUB, 2 * MW), lambda c: (0, jnp.maximum(c * (L // SUB) - 1, 0), 0))
    st = lambda shape: pl.BlockSpec((bsz,) + shape, lambda c: (0,) * (len(shape) + 1))
    out, c_fin, n_fin, m_fin = pl.pallas_call(
        _mlstm_prompt_body,
        grid=(nck,),
        in_specs=[blk(2 * MW), halo, blk(MW), blk(MW), blk(LANES), _const_spec((QK_CONV, 2 * MW)), _const_spec((1, MW))],
        out_specs=[blk(MW), st((MH, MD, MD)), st((SUB, MD)), st((SUB, LANES))],
        out_shape=[jax.ShapeDtypeStruct((bsz, seq, MW), BF16),
                   jax.ShapeDtypeStruct((bsz, MH, MD, MD), F32),
                   jax.ShapeDtypeStruct((bsz, SUB, MD), F32),
                   jax.ShapeDtypeStruct((bsz, SUB, LANES), F32)],
        scratch_shapes=[pltpu.VMEM((bsz, SUB + L, 2 * MW), F32), pltpu.VMEM((bsz, MH, MD, MD), F32),
                        pltpu.VMEM((bsz, SUB, MD), F32), pltpu.VMEM((bsz, SUB, LANES), F32)],
        compiler_params=_params(("arbitrary",)),
        name="mlstm_prompt",
    )(as3(qk), as3(qk), as3(vm), as3(om), as3(gates), wconv, gmh)
    return out.reshape(bsz * seq, MW), c_fin, n_fin, m_fin


def _mlstm_sample_body(qk_ref, buf_ref, vm_ref, om_ref, g_ref, mrow_ref, nrow_ref, c_in, n_in, wc_ref, gmh_ref,
                       out_ref, c_out, n_out, mrow_out,
                       q_scr, k_scr, wv_scr, inter_scr, dec_scr, *, t_new):
    L = qk_ref.shape[0]
    nseg = L // t_new
    cur = qk_ref[...].reshape(nseg, t_new, 2 * MW)
    buf = buf_ref[...].reshape(nseg, t_new, 2 * MW)
    tpos = lax.broadcasted_iota(jnp.int32, cur.shape, 1)
    w = lambda j: wc_ref[j:j + 1, :].reshape(1, 1, 2 * MW)
    conv = w(QK_CONV - 1) * cur
    for d in range(1, QK_CONV):
        shifted = jnp.where(tpos >= d, pltpu.roll(cur, d, axis=1), pltpu.roll(buf, d, axis=1))
        conv = conv + w(QK_CONV - 1 - d) * shifted
    qk = _silu(conv).reshape(L, 2 * MW)

    row = lax.broadcasted_iota(jnp.int32, (L, L), 0)
    colm = lax.broadcasted_iota(jnp.int32, (L, L), 1)
    smask = (row // t_new) == (colm // t_new)
    cmask = smask & (colm <= row)
    tri = jnp.where(cmask, 1.0, 0.0).astype(BF16)
    ones = jnp.where(smask, 1.0, 0.0).astype(BF16)
    seg16 = jnp.where(lax.broadcasted_iota(jnp.int32, (nseg, L), 0)
                      == lax.broadcasted_iota(jnp.int32, (nseg, L), 1) // t_new, 1.0, 0.0).astype(BF16)

    gates = g_ref[...]
    cum = _dot3_rhs(tri, gates)
    tot = _dot3_rhs(ones, gates)
    g_t, cum_t, tot_t = gates.T, cum.T, tot.T
    lane = lax.broadcasted_iota(jnp.int32, (L, LANES), 1)

    head = []
    dec_all = jnp.zeros((L, LANES), F32)
    mnew_all = jnp.zeros((L, LANES), F32)
    for h in range(MH):
        sl = slice(h * MD, (h + 1) * MD)
        q = qk[:, sl]
        k = qk[:, MW + h * MD:MW + (h + 1) * MD] * (MD ** -0.5)
        v_bf = vm_ref[:, sl]
        mp = mrow_ref[:, h:h + 1]
        dmat, sc, m_t, m_new, w_col, decay = _mlstm_gate_math(h, gates, g_t, cum, cum_t, tot, tot_t, mp, cmask, smask)
        q_scr[h] = q
        k_scr[h] = k.astype(BF16)
        wv_scr[h] = w_col * v_bf.astype(F32)
        dec_all = jnp.where(lane == h, jnp.broadcast_to(decay, (L, LANES)), dec_all)
        mnew_all = jnp.where(lane == h, jnp.broadcast_to(m_new, (L, LANES)), mnew_all)
        n_add = _dot3_rhs(seg16, w_col * k)
        dec_seg = _dot3_rhs(seg16, decay * (1.0 / t_new) * jnp.ones((L, MD), F32))
        n_out[h] = dec_seg * n_in[h] + n_add
        qn = jnp.sum(q * nrow_ref[:, sl], axis=-1, keepdims=True)
        head.append((dmat, sc, m_t, qn))
    dec_scr[...] = dec_all
    mrow_out[...] = mnew_all

    rowseg = lax.broadcasted_iota(jnp.int32, (L, MD), 0) // t_new

    def seq_body(j, carry):
        r0 = pl.multiple_of(j * t_new, t_new)
        dgrp = dec_scr[pl.ds(r0, t_new), :]
        for h in range(MH):
            c_prev = c_in[j, h]
            q_j = q_scr[h, pl.ds(r0, t_new), :].astype(BF16)
            inter_scr[h, pl.ds(r0, t_new), :] = _dot_nt(q_j, c_prev.astype(BF16))
            wv_j = jnp.where(rowseg == j, wv_scr[h], 0.0).astype(BF16)
            c_out[j, h] = dgrp[0:1, h:h + 1] * c_prev + _dot_tn(wv_j, k_scr[h])
        return carry

    lax.fori_loop(0, nseg, seq_body, 0)

    for h in range(MH):
        sl = slice(h * MD, (h + 1) * MD)
        dmat, sc, m_t, qn = head[h]
        out_ref[:, sl] = _mlstm_head_out(q_scr[h].astype(BF16), k_scr[h], vm_ref[:, sl], dmat, sc, m_t, inter_scr[h], qn,
                                         om_ref[:, sl], gmh_ref[:, sl])


def _mlstm_sample(qk, bufpad, vm, om, gates, mrows, nrows, c_state, n_state_t, wconv, gmh, *, t_new):
    L = MD
    r = qk.shape[0]
    nseg = L // t_new
    blk = lambda n: pl.BlockSpec((L, n), lambda i: (i, 0))
    return pl.pallas_call(
        functools.partial(_mlstm_sample_body, t_new=t_new),
        grid=(r // L,),
        in_specs=[blk(2 * MW), blk(2 * MW), blk(MW), blk(MW), blk(LANES), blk(LANES), blk(MW),
                  pl.BlockSpec((nseg, MH, MD, MD), lambda i: (i, 0, 0, 0)),
                  pl.BlockSpec((MH, nseg, MD), lambda i: (0, i, 0)),
                  _const_spec((QK_CONV, 2 * MW)), _const_spec((1, MW))],
        out_specs=[blk(MW), pl.BlockSpec((nseg, MH, MD, MD), lambda i: (i, 0, 0, 0)),
                   pl.BlockSpec((MH, nseg, MD), lambda i: (0, i, 0)), blk(LANES)],
        out_shape=[jax.ShapeDtypeStruct((r, MW), BF16), jax.ShapeDtypeStruct(c_state.shape, F32),
                   jax.ShapeDtypeStruct(n_state_t.shape, F32), jax.ShapeDtypeStruct((r, LANES), F32)],
        scratch_shapes=[pltpu.VMEM((MH, L, MD), F32), pltpu.VMEM((MH, L, MD), BF16), pltpu.VMEM((MH, L, MD), F32),
                        pltpu.VMEM((MH, L, MD), F32), pltpu.VMEM((L, LANES), F32)],
        compiler_params=_params(("arbitrary",)),
        name="mlstm_sample",
    )(qk, bufpad, vm, om, gates, mrows, nrows, c_state, n_state_t, wconv, gmh)


def _fox_prompt_body(qa_ref, ka_ref, va_ref, o_ref, m_scr, acc_scr, s0_scr, s1_scr, *, tq):
    i = pl.program_id(2)
    row = lax.broadcasted_iota(jnp.int32, (tq, tq), 0)
    colm = lax.broadcasted_iota(jnp.int32, (tq, tq), 1)
    nheads = m_scr.shape[0]
    m_scr[...] = jnp.full(m_scr.shape, NEG, F32)
    acc_scr[...] = jnp.zeros_like(acc_scr)

    def rows(j):
        return pl.ds(pl.multiple_of(j * tq, tq), tq)

    def produce(j, s_scr):
        for hh in range(nheads):
            sl = slice(hh * LANES, (hh + 1) * LANES)
            s_scr[hh] = _dot_nt(qa_ref[:, sl], ka_ref[rows(j), sl])

    def consume(j, s_scr, masked):
        for hh in range(nheads):
            sl = slice(hh * LANES, (hh + 1) * LANES)
            s = s_scr[hh]
            if masked:
                s = jnp.where(colm <= row, s, NEG)
            m_old = m_scr[hh, :, 0:1]
            m_new = jnp.maximum(m_old, jnp.max(s, axis=-1, keepdims=True))
            p = jnp.exp(s - m_new)
            acc_scr[hh] = jnp.exp(m_old - m_new) * acc_scr[hh] + _dot(p.astype(BF16), va_ref[rows(j), sl])
            m_scr[hh] = jnp.broadcast_to(m_new, (tq, LANES))

    produce(0, s0_scr)

    def pair_body(t, carry):
        produce(2 * t + 1, s1_scr)
        consume(2 * t, s0_scr, False)
        produce(2 * t + 2, s0_scr)
        consume(2 * t + 1, s1_scr, False)
        return carry

    lax.fori_loop(0, i // 2, pair_body, 0)

    @pl.when(i % 2 == 0)
    def _():
        consume(i, s0_scr, True)

    @pl.when(i % 2 == 1)
    def _():
        produce(i, s1_scr)
        consume(i - 1, s0_scr, False)
        consume(i, s1_scr, True)

    outs = []
    for hh in range(nheads):
        acc = acc_scr[hh]
        outs.append(acc[:, 0:FD] / acc[:, FD:FD + 1])
    o_ref[...] = jnp.concatenate(outs, axis=1)


def _fox_prompt(qa, ka, va, *, bsz, seq, tq):
    nq = seq // tq
    w2 = 2 * LANES
    return pl.pallas_call(
        functools.partial(_fox_prompt_body, tq=tq),
        grid=(bsz, FH // 2, nq),
        in_specs=[pl.BlockSpec((tq, w2), lambda b, hp, i: (b * nq + i, hp)),
                  pl.BlockSpec((seq, w2), lambda b, hp, i: (b, hp)),
                  pl.BlockSpec((seq, w2), lambda b, hp, i: (b, hp))],
        out_specs=pl.BlockSpec((tq, 2 * FD), lambda b, hp, i: (b * nq + i, hp)),
        out_shape=jax.ShapeDtypeStruct((bsz * seq, FW), F32),
        scratch_shapes=[pltpu.VMEM((2, tq, LANES), F32), pltpu.VMEM((2, tq, LANES), F32),
                        pltpu.VMEM((2, tq, tq), F32), pltpu.VMEM((2, tq, tq), F32)],
        compiler_params=_params(("arbitrary", "arbitrary", "arbitrary")),
        name="fox_prompt",
    )(qa, ka, va)


def _fox_sample_body(pt_ref, q_ref, kn_ref, vn_ref, cn_ref, rep_ref, msuf_ref, msel_ref, lf_hbm, k_hbm, v_hbm,
                     o_ref,
                     kbuf, vbuf, lfbuf, bias_scr, m_scr, l_scr, acc_scr, ksem, vsem, lfsem,
                     *, nseq, npg, ch, t_new):
    b = pl.program_id(0)
    nch = npg // ch
    nkey = ch * LANES
    nrow = FH * t_new

    def kv_copies(seq, c, slot):
        cps = []
        for pi in range(ch):
            pg = pt_ref[seq, c * ch + pi]
            cps.append(pltpu.make_async_copy(k_hbm.at[pg], kbuf.at[slot, pi], ksem.at[slot]))
            cps.append(pltpu.make_async_copy(v_hbm.at[pg], vbuf.at[slot, pi], vsem.at[slot]))
        return cps

    def chunk_tile(buf, slot):
        return jnp.concatenate([buf[slot, pi].reshape(FW, LANES) for pi in range(ch)], axis=1).astype(BF16)

    def lf_copy(seq, p, slot):
        return pltpu.make_async_copy(lf_hbm.at[pt_ref[seq, p]], lfbuf.at[slot, p], lfsem.at[slot])

    def lf_start(seq, slot):
        def body(p, carry):
            lf_copy(seq, p, slot).start()
            return carry
        lax.fori_loop(0, npg, body, 0)

    def lf_wait(seq, slot):
        def body(p, carry):
            lf_copy(seq, p, slot).wait()
            return carry
        lax.fori_loop(0, npg, body, 0)

    @pl.when(b == 0)
    def _():
        lf_start(0, 0)
        for cp in kv_copies(0, 0, 0):
            cp.start()

    lslot = b % 2
    lf_wait(b, lslot)

    @pl.when(b + 1 < nseq)
    def _():
        lf_start(b + 1, 1 - lslot)

    lf2 = lfbuf[lslot].reshape(npg * FH, LANES)
    suf_in = _dot3_lhs(lf2, msuf_ref[...])
    cross = jnp.sum(_dot3_rhs(msel_ref[...], lf2), axis=-1, keepdims=True)
    bias_scr[...] = (suf_in + cross).reshape(npg, FH, LANES)

    q = q_ref[0].astype(F32)
    rr = lax.broadcasted_iota(jnp.int32, (nrow, FW), 0) // t_new
    cc = lax.broadcasted_iota(jnp.int32, (nrow, FW), 1) // FD
    qbd = jnp.where(rr == cc, jnp.concatenate([q] * FH, axis=0), 0.0).astype(BF16)
    rep = rep_ref[...]

    def bias_rows(bias8):
        n = bias8.shape[1]
        a, b2, c2 = _split3(bias8)
        return jnp.concatenate([a.astype(F32), b2.astype(F32), c2.astype(F32),
                                jnp.zeros((LANES - 3 * FH, n), F32)], axis=0).astype(BF16)

    pad = jnp.zeros((LANES - t_new, FW), F32)
    kn = jnp.concatenate([kn_ref[0], pad], axis=0).astype(BF16)
    vn = jnp.concatenate([vn_ref[0], pad], axis=0).astype(BF16)
    s = _dot_nt(qbd, kn) + _dot(rep, bias_rows(-cn_ref[0]))
    qpos = lax.broadcasted_iota(jnp.int32, (nrow, LANES), 0) % t_new
    kpos = lax.broadcasted_iota(jnp.int32, (nrow, LANES), 1)
    s = jnp.where(kpos <= qpos, s, NEG)
    m0 = jnp.max(s, axis=-1, keepdims=True)
    p = jnp.exp(s - m0)
    m_scr[...] = jnp.broadcast_to(m0, m_scr.shape)
    l_scr[...] = jnp.broadcast_to(jnp.sum(p, axis=-1, keepdims=True), l_scr.shape)
    acc_scr[...] = _dot(p.astype(BF16), vn)

    def chunk_body(c, carry):
        g = b * nch + c
        slot = g % 2
        for cp in kv_copies(b, c, slot):
            cp.wait()

        @pl.when(g + 1 < nseq * nch)
        def _():
            wrap = c + 1 == nch
            nb = jnp.where(wrap, b + 1, b)
            ncx = jnp.where(wrap, 0, c + 1)
            for cp in kv_copies(nb, ncx, 1 - slot):
                cp.start()

        kt = chunk_tile(kbuf, slot)
        blk = bias_scr[pl.ds(c * ch, ch)]
        bias8 = jnp.concatenate([blk[pi] for pi in range(ch)], axis=1)
        s = _dot(qbd, kt) + _dot(rep, bias_rows(bias8))
        m_old = m_scr[:, 0:1]
        m_new = jnp.maximum(m_old, jnp.max(s, axis=-1, keepdims=True))
        p = jnp.exp(s - m_new)
        alpha = jnp.exp(m_old - m_new)
        vt = chunk_tile(vbuf, slot)
        acc_scr[...] = alpha * acc_scr[...] + _dot_nt(p.astype(BF16), vt)
        l_scr[...] = alpha * l_scr[...] + jnp.sum(p, axis=-1, keepdims=True)
        m_scr[...] = jnp.broadcast_to(m_new, m_scr.shape)
        return carry

    lax.fori_loop(0, nch, chunk_body, 0)

    accn = acc_scr[...] / l_scr[:, 0:1]
    out = jnp.zeros((t_new, FW), F32)
    hcol = lax.broadcasted_iota(jnp.int32, (t_new, FW), 1) // FD
    for h in range(FH):
        out = jnp.where(hcol == h, accn[h * t_new:(h + 1) * t_new, :], out)
    o_ref[0] = out


def _fox_sample(page_table, q_s, k_new, v_new, c_new, rep, msuf, msel, lf_t, k_t, v_t, *, ch):
    nseq, t_new, _ = q_s.shape
    npg = page_table.shape[1]
    seqblk = lambda n: pl.BlockSpec((1, t_new, n), lambda b, pt: (b, 0, 0))
    cst = lambda shape: pl.BlockSpec(shape, lambda b, pt: (0,) * len(shape), pipeline_mode=pl.Buffered(1))
    anyspec = pl.BlockSpec(memory_space=pl.ANY)
    nrow = FH * t_new
    grid_spec = pltpu.PrefetchScalarGridSpec(
        num_scalar_prefetch=1,
        grid=(nseq,),
        in_specs=[seqblk(FW), seqblk(FW), seqblk(FW), pl.BlockSpec((1, FH, LANES), lambda b, pt: (b, 0, 0)),
                  cst((nrow, LANES)), cst((LANES, LANES)), cst((npg * FH, npg * FH)), anyspec, anyspec, anyspec],
        out_specs=seqblk(FW),
        scratch_shapes=[pltpu.VMEM((2, ch, FH, FD, LANES), F32), pltpu.VMEM((2, ch, FH, FD, LANES), F32),
                        pltpu.VMEM((2, npg, FH, LANES), F32), pltpu.VMEM((npg, FH, LANES), F32),
                        pltpu.VMEM((nrow, LANES), F32), pltpu.VMEM((nrow, LANES), F32), pltpu.VMEM((nrow, FW), F32),
                        pltpu.SemaphoreType.DMA((2,)), pltpu.SemaphoreType.DMA((2,)), pltpu.SemaphoreType.DMA((2,))],
    )
    return pl.pallas_call(
        functools.partial(_fox_sample_body, nseq=nseq, npg=npg, ch=ch, t_new=t_new),
        grid_spec=grid_spec,
        out_shape=jax.ShapeDtypeStruct((nseq, t_new, FW), F32),
        compiler_params=_params(("arbitrary",)),
        name="fox_sample",
    )(page_table, q_s, k_new, v_new, c_new, rep, msuf, msel, lf_t, k_t, v_t)


def _ffn_body(x_ref, om_ref, af_ref, of_ref, buf_ref, wo_ref, gf_ref, wu_ref, wc_ref, bc_ref, wd_ref,
              y_ref, tail_ref, ext, *, tiles_per_seq, t_new):
    tm = x_ref.shape[0]
    mixin = jnp.concatenate([om_ref[...], (of_ref[...] * af_ref[...]).astype(BF16)], axis=1)
    h = x_ref[...] + _dot(mixin, wo_ref[...])
    hn = (h * lax.rsqrt(jnp.mean(h * h, axis=-1, keepdims=True) + EPS) * gf_ref[...]).astype(BF16)
    up_a = _dot(hn, wu_ref[:, 0:D_FF])
    up_g = _dot(hn, wu_ref[:, D_FF:2 * D_FF])
    if t_new is None:
        @pl.when(pl.program_id(0) % tiles_per_seq == 0)
        def _():
            ext[0:SUB, :] = jnp.zeros((SUB, D_FF), F32)
        ext[SUB:SUB + tm, :] = up_a
        conv = wc_ref[0:1, :] * ext[pl.ds(SUB - 2, tm), :]
        for j in range(1, FFN_CONV):
            conv = conv + wc_ref[j:j + 1, :] * ext[pl.ds(SUB - 2 + j, tm), :]
        ext[0:SUB, :] = up_a[tm - SUB:tm, :]
        tail_ref[...] = up_a[tm - SUB:tm, :]
    else:
        nseg = tm // t_new
        cur = up_a.reshape(nseg, t_new, D_FF)
        buf = buf_ref[...].reshape(nseg, t_new, D_FF)
        tpos = lax.broadcasted_iota(jnp.int32, cur.shape, 1)
        w = lambda j: wc_ref[j:j + 1, :].reshape(1, 1, D_FF)
        conv = w(FFN_CONV - 1) * cur
        for d in range(1, FFN_CONV):
            shifted = jnp.where(tpos >= d, pltpu.roll(cur, d, axis=1), pltpu.roll(buf, d, axis=1))
            conv = conv + w(FFN_CONV - 1 - d) * shifted
        conv = conv.reshape(tm, D_FF)
        tail_ref[...] = up_a
    act = (_silu(conv + bc_ref[...]) * up_g).astype(BF16)
    y_ref[...] = h + _dot(act, wd_ref[...])


def _ffn(x2d, outm, af, of, bufpad, w_out, g_ffn, w_up, w_ffconv, b_ffconv, w_down, *, tm, tiles_per_seq, t_new):
    r = x2d.shape[0]
    row = lambda n: pl.BlockSpec((tm, n), lambda i: (i, 0))
    if t_new is None:
        tail_spec = pl.BlockSpec((SUB, D_FF), lambda i: (i, 0))
        tail_shape = jax.ShapeDtypeStruct((r // tm * SUB, D_FF), F32)
        buf_spec = pl.BlockSpec((SUB, D_FF), lambda i: (0, 0))
    else:
        tail_spec = row(D_FF)
        tail_shape = jax.ShapeDtypeStruct((r, D_FF), F32)
        buf_spec = row(D_FF)
    return pl.pallas_call(
        functools.partial(_ffn_body, tiles_per_seq=tiles_per_seq, t_new=t_new),
        grid=(r // tm,),
        in_specs=[row(D_MODEL), row(MW), row(FW), row(FW), buf_spec,
                  _const_spec((D_MODEL, D_MODEL)), _const_spec((1, D_MODEL)), _const_spec((D_MODEL, 2 * D_FF)),
                  _const_spec((FFN_CONV, D_FF)), _const_spec((1, D_FF)), _const_spec((D_FF, D_MODEL))],
        out_specs=[row(D_MODEL), tail_spec],
        out_shape=[jax.ShapeDtypeStruct((r, D_MODEL), F32), tail_shape],
        scratch_shapes=[pltpu.VMEM((SUB + tm, D_FF), F32)],
        compiler_params=_params(("arbitrary",)),
        name="ffn",
    )(x2d, outm, af, of, bufpad, w_out, g_ffn, w_up, w_ffconv, b_ffconv, w_down)


def _head_slots(w):
    d = w.shape[0]
    return jnp.pad(w.reshape(d, FH, FD), ((0, 0), (0, 0), (0, LANES - FD))).reshape(d, FH * LANES)


def _lane_pad(v, n=LANES):
    return jnp.pad(v, ((0, 0), (0, n - v.shape[1])))


def _state_rows(state, t_new):
    nseq, k1, c = state.shape
    return jnp.pad(state, ((0, 0), (t_new - k1, 0), (0, 0))).reshape(nseq * t_new, c)


def _unslot(a, lead):
    return a.reshape(lead + (FH, LANES))[..., :FD]


def kernel(x_prompt, x_sample, cache_k, cache_v, cache_logf, page_table, state_C, state_n, state_m, state_qkconv,
           state_ffnconv, g_mix, w_in, b_gate, w_qkconv, g_qf, g_kf, g_mh, w_out, g_ffn, w_up, w_ffconv, b_ffconv,
           w_down):
    assert w_in.shape[0] == 1, "single-layer kernel"
    bsz, seq, _ = x_prompt.shape
    dbsz, t_new, _ = x_sample.shape
    tm = 256

    w = w_in[0]
    w_pad = jnp.concatenate(
        [w[:, 0:O_QF], _head_slots(w[:, O_QF:O_KF]), _head_slots(w[:, O_KF:O_VF]), _head_slots(w[:, O_VF:O_OF]),
         w[:, O_OF:O_GATE], _lane_pad(w[:, O_GATE:])], axis=1).astype(BF16)
    bg_pad = _lane_pad(b_gate.astype(F32))
    gq_pad = _lane_pad(g_qf.astype(F32))
    gk_pad = _lane_pad(g_kf.astype(F32))
    gmix = g_mix.astype(F32)
    wconv = w_qkconv[0].astype(F32)
    gmh = g_mh.astype(F32)
    wo_b, wu_b, wd_b = w_out[0].astype(BF16), w_up[0].astype(BF16), w_down[0].astype(BF16)
    gffn, wfc, bfc = g_ffn.astype(F32), w_ffconv[0].astype(F32), b_ffconv.astype(F32)

    ridx = jnp.arange(tm)
    tri_p = (ridx[None, :] <= ridx[:, None]).astype(BF16)
    tri_s = ((ridx[None, :] <= ridx[:, None]) & (ridx[None, :] // t_new == ridx[:, None] // t_new)).astype(BF16)

    xp = x_prompt.reshape(bsz * seq, D_MODEL).astype(F32)
    (qk_p, vm_p, om_p, qa_p, ka_p, kp_p, va_p, vp_p, of_p, g_p, _) = _proj(
        xp, gmix, w_pad, bg_pad, gq_pad, gk_pad, tri_p, tm=tm, tiles_per_seq=seq // tm)
    outm_p, c_p, n_p, m_p = _mlstm_prompt(qk_p, vm_p, om_p, g_p, wconv, gmh, bsz=bsz, seq=seq)
    af_p = _fox_prompt(qa_p, ka_p, va_p, bsz=bsz, seq=seq, tq=512)
    y_p, tail_p = _ffn(xp, outm_p, af_p, of_p, jnp.zeros((SUB, D_FF), F32), wo_b, gffn, wu_b, wfc, bfc, wd_b,
                       tm=tm, tiles_per_seq=seq // tm, t_new=None)

    xs = x_sample.reshape(dbsz * t_new, D_MODEL).astype(F32)
    (qk_s, vm_s, om_s, qa_s, _, kp_s, _, vp_s, of_s, g_s, fc_s) = _proj(
        xs, gmix, w_pad, bg_pad, gq_pad, gk_pad, tri_s, tm=tm, tiles_per_seq=1)
    mrows = _lane_pad(jnp.repeat(state_m[0].astype(F32), t_new, axis=0))
    nrows = jnp.repeat(state_n[0].astype(F32).reshape(dbsz, MW), t_new, axis=0)
    n_t = jnp.transpose(state_n[0].astype(F32), (1, 0, 2))
    outm_s, c_s, n_s_t, mrow_s = _mlstm_sample(
        qk_s, _state_rows(state_qkconv[0].astype(F32), t_new), vm_s, om_s, g_s, mrows, nrows,
        state_C[0].astype(F32), n_t, wconv, gmh, t_new=t_new)

    k_new = _unslot(kp_s, (dbsz, t_new))
    v_new = _unslot(vp_s, (dbsz, t_new))
    q_s = _unslot(qa_s, (dbsz, t_new)).reshape(dbsz, t_new, FW)
    c_new = _lane_pad(jnp.transpose(fc_s.reshape(dbsz, t_new, LANES)[:, :, G_FF:G_FF + FH], (0, 2, 1))
                      .reshape(dbsz * FH, t_new)).reshape(dbsz, FH, LANES)
    npg = page_table.shape[1]
    nrow = FH * t_new
    rr = jnp.arange(nrow)[:, None] // t_new
    cc = jnp.arange(LANES)[None, :]
    rep = ((cc < 3 * FH) & (cc % FH == rr)).astype(BF16)
    kk = jnp.arange(LANES)
    msuf = (kk[:, None] > kk[None, :]).astype(BF16)
    pr = jnp.arange(npg * FH)
    msel = ((pr[None, :] % FH == pr[:, None] % FH) & (pr[None, :] // FH > pr[:, None] // FH)).astype(BF16)
    k_t = jnp.transpose(cache_k[0], (0, 2, 3, 1)).astype(F32)
    v_t = jnp.transpose(cache_v[0], (0, 2, 3, 1)).astype(F32)
    lf_t = jnp.transpose(cache_logf[0], (0, 2, 1)).astype(F32)
    af_s = _fox_sample(page_table.astype(jnp.int32), q_s, k_new.reshape(dbsz, t_new, FW),
                       v_new.reshape(dbsz, t_new, FW), c_new, rep, msuf, msel, lf_t, k_t, v_t, ch=16)
    y_s, tail_s = _ffn(xs, outm_s, af_s.reshape(dbsz * t_new, FW), of_s,
                       _state_rows(state_ffnconv[0].astype(F32), t_new), wo_b, gffn, wu_b, wfc, bfc, wd_b,
                       tm=tm, tiles_per_seq=1, t_new=t_new)

    dt = x_prompt.dtype
    y_prompt = y_p.reshape(bsz, seq, D_MODEL).astype(dt)
    y_sample = y_s.reshape(dbsz, t_new, D_MODEL).astype(x_sample.dtype)
    k_prompt = _unslot(kp_p, (bsz, seq))[None]
    v_prompt = _unslot(vp_p, (bsz, seq))[None]
    logf_prompt = g_p.reshape(bsz, seq, LANES)[:, :, G_FF:G_FF + FH][None]
    c_prompt = c_p[None]
    n_prompt = n_p[:, :MH, :][None]
    m_prompt = m_p[:, :MH, 0][None]
    qkconv_prompt = qk_p.reshape(bsz, seq, 2 * MW)[:, seq - (QK_CONV - 1):, :][None]
    ffnconv_prompt = tail_p.reshape(bsz, seq // tm, SUB, D_FF)[:, -1, SUB - (FFN_CONV - 1):, :][None]
    k_sample = k_new[None]
    v_sample = v_new[None]
    logf_sample = g_s.reshape(dbsz, t_new, LANES)[:, :, G_FF:G_FF + FH][None]
    c_sample = c_s[None]
    n_sample = jnp.transpose(n_s_t, (1, 0, 2))[None]
    m_sample = mrow_s.reshape(dbsz, t_new, LANES)[:, t_new - 1, :MH][None]
    qkconv_sample = qk_s.reshape(dbsz, t_new, 2 * MW)[:, t_new - (QK_CONV - 1):, :][None]
    ffnconv_sample = tail_s.reshape(dbsz, t_new, D_FF)[:, t_new - (FFN_CONV - 1):, :][None]
    return (y_prompt, y_sample, k_prompt, v_prompt, logf_prompt, c_prompt, n_prompt, m_prompt, qkconv_prompt,
            ffnconv_prompt, k_sample, v_sample, logf_sample, c_sample, n_sample, m_sample, qkconv_sample,
            ffnconv_sample)
```

```python
import functools

import jax
import jax.numpy as jnp
from jax import lax
from jax.experimental import pallas as pl
from jax.experimental.pallas import tpu as pltpu

F32, BF16 = jnp.float32, jnp.bfloat16

D_MODEL = 1024
MH, MD = 4, 128
MW = MH * MD
FH, FD = 8, 64
FW = FH * FD
QK_CONV, FFN_CONV = 4, 3
D_FF = 2816
N_GATES = 2 * MH + FH
EPS = 1e-6
NEG = -1e30
LANES = 128
SUB = 8
VMEM_LIMIT = 56 * 1024 * 1024

C_QK, C_VM, C_OM = 0, 2 * MW, 3 * MW
C_QF = 4 * MW
C_KF, C_VF, C_OF, C_G = C_QF + FW, C_QF + 2 * FW, C_QF + 3 * FW, C_QF + 4 * FW
NC = C_G + LANES
G_FM, G_FF = MH, 2 * MH


def _split3(x):
    a = x.astype(BF16)
    r = x - a.astype(F32)
    b = r.astype(BF16)
    c = (r - b.astype(F32)).astype(BF16)
    return a, b, c


def _dot(a, b):
    return jnp.dot(a, b, preferred_element_type=F32)


def _dot_nt(a, b):
    return lax.dot_general(a, b, (((1,), (1,)), ((), ())), preferred_element_type=F32)


def _dot_tn(a, b):
    return lax.dot_general(a, b, (((0,), (0,)), ((), ())), preferred_element_type=F32)


def _dot3_rhs(m, x):
    a, b, c = _split3(x)
    return _dot(m, c) + _dot(m, b) + _dot(m, a)


def _dot3_lhs(x, m):
    a, b, c = _split3(x)
    return _dot(c, m) + _dot(b, m) + _dot(a, m)


def _log_sigmoid(x):
    return jnp.minimum(x, 0.0) - jnp.log1p(jnp.exp(-jnp.abs(x)))


def _silu(x):
    return x * jax.nn.sigmoid(x)


def _const_spec(shape):
    n = len(shape)
    return pl.BlockSpec(shape, lambda *_: (0,) * n, pipeline_mode=pl.Buffered(1))


def _params(sem):
    return pltpu.CompilerParams(dimension_semantics=sem, vmem_limit_bytes=VMEM_LIMIT)


def _proj_body(x_ref, gmix_ref, w_ref, bg_ref, gq_ref, gk_ref, tri_ref,
               qk_ref, vm_ref, om_ref, qa_ref, ka_ref, va_ref, of_ref, g_ref, fc_ref, kf_ref, vf_ref,
               carry_ref, *, tiles_per_seq, kv_transposed):
    tm = x_ref.shape[0]
    if tiles_per_seq > 1:
        @pl.when(pl.program_id(0) % tiles_per_seq == 0)
        def _():
            carry_ref[...] = jnp.zeros_like(carry_ref)

    x = x_ref[...]
    ms = jnp.mean(x * x, axis=-1, keepdims=True)
    xn = (x * lax.rsqrt(ms + EPS) * gmix_ref[...]).astype(BF16)

    def proj(lo, n):
        return _dot(xn, w_ref[:, lo:lo + n])

    g = proj(C_G, LANES) + bg_ref[...]
    col = lax.broadcasted_iota(jnp.int32, g.shape, 1)
    gates = jnp.where(col < MH, g, jnp.where(col < N_GATES, _log_sigmoid(g), 0.0))
    g_ref[...] = gates

    cum = _dot3_rhs(tri_ref[...], gates)
    if tiles_per_seq > 1:
        cum = cum + carry_ref[...]
        carry_ref[...] = cum[tm - 1:tm, :]
    fc_ref[...] = cum

    nf = -cum
    n1 = nf.astype(BF16).astype(F32)
    r1 = nf - n1
    n2 = r1.astype(BF16).astype(F32)
    n3 = (r1 - n2).astype(BF16).astype(F32)

    pq = proj(C_QF, FW)
    pk = proj(C_KF, FW)
    pv = proj(C_VF, FW)
    qk_ref[...] = proj(C_QK, 2 * MW)
    vm_ref[...] = proj(C_VM, MW).astype(BF16)
    om_ref[...] = jax.nn.sigmoid(proj(C_OM, MW))
    of_ref[...] = jax.nn.sigmoid(proj(C_OF, FW))
    lane = lax.broadcasted_iota(jnp.int32, (tm, LANES), 1)
    low = lane < FD
    gq = gq_ref[...]
    gk = gk_ref[...]

    def pair_norm(x, g2):
        sq = x * x
        ms_lo = jnp.sum(jnp.where(low, sq, 0.0), axis=-1, keepdims=True) * (1.0 / FD)
        ms_hi = jnp.sum(jnp.where(low, 0.0, sq), axis=-1, keepdims=True) * (1.0 / FD)
        return x * jnp.where(low, lax.rsqrt(ms_lo + EPS), lax.rsqrt(ms_hi + EPS)) * g2

    def store_f32(ref, j, pair):
        if kv_transposed:
            tr = jnp.concatenate([pair[r0:r0 + LANES, :].T for r0 in range(0, tm, LANES)], axis=1)
            ref[0, 2 * j] = tr[0:FD, :]
            ref[0, 2 * j + 1] = tr[FD:2 * FD, :]
        else:
            ref[:, j * LANES:(j + 1) * LANES] = pair

    for j in range(FH // 2):
        psl = slice(j * LANES, (j + 1) * LANES)
        qn2 = pair_norm(pq[:, psl], gq) * (FD ** -0.5)
        kn2 = pair_norm(pk[:, psl], gk)
        xv2 = pv[:, psl]
        store_f32(kf_ref, j, kn2)
        store_f32(vf_ref, j, xv2)
        for odd in range(2):
            h = 2 * j + odd
            sl = slice(h * LANES, (h + 1) * LANES)
            qn, kn, xv = ((pltpu.roll(a, FD, axis=1) if odd else a) for a in (qn2, kn2, xv2))
            c = G_FF + h
            b1 = jnp.broadcast_to(n1[:, c:c + 1], (tm, LANES))
            b2 = jnp.broadcast_to(n2[:, c:c + 1], (tm, LANES))
            b3 = jnp.broadcast_to(n3[:, c:c + 1], (tm, LANES))
            zero = jnp.zeros((tm, LANES), F32)
            qa = jnp.where(low, qn, jnp.where(lane < FD + 3, 1.0, zero))
            ka = jnp.where(low, kn, jnp.where(lane == FD, b1, jnp.where(lane == FD + 1, b2,
                                                                       jnp.where(lane == FD + 2, b3, zero))))
            va = jnp.where(low, xv, jnp.where(lane == FD, 1.0, zero))
            qa_ref[:, sl] = qa.astype(BF16)
            ka_ref[:, sl] = ka.astype(BF16)
            va_ref[:, sl] = va.astype(BF16)


def _proj(x2d, gmix, w_pad, bg_pad, gq2, gk2, tri, *, tm, tiles_per_seq, kv_transposed):
    r = x2d.shape[0]
    row = lambda n: pl.BlockSpec((tm, n), lambda i: (i, 0))
    widths = [(2 * MW, F32), (MW, BF16), (MW, F32), (FH * LANES, BF16), (FH * LANES, BF16), (FH * LANES, BF16),
              (FW, F32), (LANES, F32), (LANES, F32)]
    out_specs = [row(n) for n, _ in widths]
    out_shape = [jax.ShapeDtypeStruct((r, n), dt) for n, dt in widths]
    if kv_transposed:
        nseq = r // (tm * tiles_per_seq)
        kv_spec = pl.BlockSpec((1, FH, FD, tm), lambda i: (i // tiles_per_seq, 0, 0, i % tiles_per_seq))
        kv_shape = jax.ShapeDtypeStruct((nseq, FH, FD, tm * tiles_per_seq), F32)
    else:
        kv_spec, kv_shape = row(FW), jax.ShapeDtypeStruct((r, FW), F32)
    return pl.pallas_call(
        functools.partial(_proj_body, tiles_per_seq=tiles_per_seq, kv_transposed=kv_transposed),
        grid=(r // tm,),
        in_specs=[row(D_MODEL), _const_spec((1, D_MODEL)), _const_spec((D_MODEL, NC)), _const_spec((1, LANES)),
                  _const_spec((1, LANES)), _const_spec((1, LANES)), _const_spec((tm, tm))],
        out_specs=out_specs + [kv_spec, kv_spec],
        out_shape=out_shape + [kv_shape, kv_shape],
        scratch_shapes=[pltpu.VMEM((1, LANES), F32)],
        compiler_params=_params(("arbitrary",)),
        name="proj",
    )(x2d, gmix, w_pad, bg_pad, gq2, gk2, tri)


def _mlstm_gate_math(h, gates, g_t, cum, cum_t, tot, tot_t, mp_col, cmask, smask):
    i_col, i_row = gates[:, h:h + 1], g_t[h:h + 1, :]
    c = G_FM + h
    b_col, b_row = cum[:, c:c + 1], cum_t[c:c + 1, :]
    t_col, t_row = tot[:, c:c + 1], tot_t[c:c + 1, :]
    logd = jnp.where(cmask, b_col + (i_row - b_row), NEG)
    m_t = jnp.maximum(b_col + mp_col, jnp.max(logd, axis=-1, keepdims=True))
    z = jnp.where(smask, (t_row - b_row) + i_row, NEG)
    m_new = jnp.maximum(t_col + mp_col, jnp.max(z, axis=-1, keepdims=True))
    dmat = jnp.exp(logd - m_t)
    sc = jnp.exp(b_col + mp_col - m_t)
    w_col = jnp.exp(t_col - b_col + i_col - m_new)
    decay = jnp.exp(t_col + mp_col - m_new)
    return dmat, sc, m_t, m_new, w_col, decay


def _mlstm_head_out(q_bf, k_bf, v_bf, dmat, sc, m_t, inter, qn, om, gmh):
    s = _dot_nt(q_bf, k_bf) * dmat
    num = _dot(s.astype(BF16), v_bf) + sc * inter
    den = jnp.sum(s, axis=-1, keepdims=True) + sc * qn
    hc = num / jnp.maximum(jnp.abs(den), jnp.exp(-m_t))
    hn = hc * lax.rsqrt(jnp.mean(hc * hc, axis=-1, keepdims=True) + EPS) * gmh
    return (om * hn).astype(BF16)


def _mlstm_prompt_body(qk_ref, halo_ref, vm_ref, om_ref, g_ref, wc_ref, gmh_ref,
                       out_ref, c_out, n_out, m_out,
                       ext, c_scr, n_scr, m_scr):
    c_idx = pl.program_id(1)
    L = qk_ref.shape[0]

    @pl.when(c_idx == 0)
    def _():
        c_scr[...] = jnp.zeros_like(c_scr)
        n_scr[...] = jnp.zeros_like(n_scr)
        m_scr[...] = jnp.zeros_like(m_scr)

    ext[0:SUB, :] = jnp.where(c_idx == 0, 0.0, halo_ref[...])
    ext[SUB:SUB + L, :] = qk_ref[...]
    conv = wc_ref[0:1, :] * ext[pl.ds(SUB - 3, L), :]
    for j in range(1, QK_CONV):
        conv = conv + wc_ref[j:j + 1, :] * ext[pl.ds(SUB - 3 + j, L), :]
    qk = _silu(conv)

    row = lax.broadcasted_iota(jnp.int32, (L, L), 0)
    colm = lax.broadcasted_iota(jnp.int32, (L, L), 1)
    cmask = colm <= row
    smask = colm >= 0
    tri = jnp.where(cmask, 1.0, 0.0).astype(BF16)
    ones = jnp.ones((L, L), BF16)

    gates = g_ref[...]
    cum = _dot3_rhs(tri, gates)
    tot = _dot3_rhs(ones, gates)
    g_t, cum_t, tot_t = gates.T, cum.T, tot.T

    for h in range(MH):
        sl = slice(h * MD, (h + 1) * MD)
        q = qk[:, sl]
        k = qk[:, MW + h * MD:MW + (h + 1) * MD] * (MD ** -0.5)
        q_bf, k_bf = q.astype(BF16), k.astype(BF16)
        v_bf = vm_ref[:, sl]
        mp = m_scr[h:h + 1, 0:1]
        dmat, sc, m_t, m_new, w_col, decay = _mlstm_gate_math(h, gates, g_t, cum, cum_t, tot, tot_t, mp, cmask, smask)
        c_prev = c_scr[h]
        n_prev = n_scr[h:h + 1, :]
        inter = _dot_nt(q_bf, c_prev.astype(BF16))
        qn = jnp.sum(q * n_prev, axis=-1, keepdims=True)
        out_ref[:, sl] = _mlstm_head_out(q_bf, k_bf, v_bf, dmat, sc, m_t, inter, qn, om_ref[:, sl], gmh_ref[:, sl])
        dec = decay[L - 1:L, :]
        wv = (w_col * v_bf.astype(F32)).astype(BF16)
        c_scr[h] = dec * c_prev + _dot_tn(wv, k_bf)
        n_scr[h:h + 1, :] = dec * n_prev + jnp.sum(w_col * k, axis=0, keepdims=True)
        m_scr[h:h + 1, :] = jnp.broadcast_to(m_new[L - 1:L, :], (1, LANES))

    @pl.when(c_idx == pl.num_programs(1) - 1)
    def _():
        c_out[0] = c_scr[...]
        n_out[0] = n_scr[...]
        m_out[0] = m_scr[...]


def _mlstm_prompt(qk, vm, om, gates, wconv, gmh, *, bsz, seq):
    L = MD
    nck = seq // L
    blk = lambda n: pl.BlockSpec((L, n), lambda b, c: (b * nck + c, 0))
    halo = pl.BlockSpec((SUB, 2 * MW), lambda b, c: (jnp.maximum((b * nck + c) * (L // SUB) - 1, 0), 0))
    st = lambda shape: pl.BlockSpec((1,) + shape, lambda b, c: (b,) + (0,) * len(shape))
    return pl.pallas_call(
        _mlstm_prompt_body,
        grid=(bsz, nck),
        in_specs=[blk(2 * MW), halo, blk(MW), blk(MW), blk(LANES), _const_spec((QK_CONV, 2 * MW)), _const_spec((1, MW))],
        out_specs=[blk(MW), st((MH, MD, MD)), st((SUB, MD)), st((SUB, LANES))],
        out_shape=[jax.ShapeDtypeStruct((bsz * seq, MW), BF16),
                   jax.ShapeDtypeStruct((bsz, MH, MD, MD), F32),
                   jax.ShapeDtypeStruct((bsz, SUB, MD), F32),
                   jax.ShapeDtypeStruct((bsz, SUB, LANES), F32)],
        scratch_shapes=[pltpu.VMEM((SUB + L, 2 * MW), F32), pltpu.VMEM((MH, MD, MD), F32),
                        pltpu.VMEM((SUB, MD), F32), pltpu.VMEM((SUB, LANES), F32)],
        compiler_params=_params(("arbitrary", "arbitrary")),
        name="mlstm_prompt",
    )(qk, qk, vm, om, gates, wconv, gmh)


def _mlstm_sample_body(qk_ref, buf_ref, vm_ref, om_ref, g_ref, mrow_ref, nrow_ref, c_in, n_in, wc_ref, gmh_ref,
                       out_ref, c_out, n_out, mrow_out,
                       q_scr, k_scr, wv_scr, inter_scr, dec_scr, *, t_new):
    L = qk_ref.shape[0]
    nseg = L // t_new
    cur = qk_ref[...].reshape(nseg, t_new, 2 * MW)
    buf = buf_ref[...].reshape(nseg, t_new, 2 * MW)
    tpos = lax.broadcasted_iota(jnp.int32, cur.shape, 1)
    w = lambda j: wc_ref[j:j + 1, :].reshape(1, 1, 2 * MW)
    conv = w(QK_CONV - 1) * cur
    for d in range(1, QK_CONV):
        shifted = jnp.where(tpos >= d, pltpu.roll(cur, d, axis=1), pltpu.roll(buf, d, axis=1))
        conv = conv + w(QK_CONV - 1 - d) * shifted
    qk = _silu(conv).reshape(L, 2 * MW)

    row = lax.broadcasted_iota(jnp.int32, (L, L), 0)
    colm = lax.broadcasted_iota(jnp.int32, (L, L), 1)
    smask = (row // t_new) == (colm // t_new)
    cmask = smask & (colm <= row)
    tri = jnp.where(cmask, 1.0, 0.0).astype(BF16)
    ones = jnp.where(smask, 1.0, 0.0).astype(BF16)
    seg16 = jnp.where(lax.broadcasted_iota(jnp.int32, (nseg, L), 0)
                      == lax.broadcasted_iota(jnp.int32, (nseg, L), 1) // t_new, 1.0, 0.0).astype(BF16)

    gates = g_ref[...]
    cum = _dot3_rhs(tri, gates)
    tot = _dot3_rhs(ones, gates)
    g_t, cum_t, tot_t = gates.T, cum.T, tot.T
    lane = lax.broadcasted_iota(jnp.int32, (L, LANES), 1)

    head = []
    dec_all = jnp.zeros((L, LANES), F32)
    mnew_all = jnp.zeros((L, LANES), F32)
    for h in range(MH):
        sl = slice(h * MD, (h + 1) * MD)
        q = qk[:, sl]
        k = qk[:, MW + h * MD:MW + (h + 1) * MD] * (MD ** -0.5)
        v_bf = vm_ref[:, sl]
        mp = mrow_ref[:, h:h + 1]
        dmat, sc, m_t, m_new, w_col, decay = _mlstm_gate_math(h, gates, g_t, cum, cum_t, tot, tot_t, mp, cmask, smask)
        q_scr[h] = q
        k_scr[h] = k.astype(BF16)
        wv_scr[h] = w_col * v_bf.astype(F32)
        dec_all = jnp.where(lane == h, jnp.broadcast_to(decay, (L, LANES)), dec_all)
        mnew_all = jnp.where(lane == h, jnp.broadcast_to(m_new, (L, LANES)), mnew_all)
        n_add = _dot3_rhs(seg16, w_col * k)
        dec_seg = _dot3_rhs(seg16, decay * (1.0 / t_new) * jnp.ones((L, MD), F32))
        n_out[h] = dec_seg * n_in[h] + n_add
        qn = jnp.sum(q * nrow_ref[:, sl], axis=-1, keepdims=True)
        head.append((dmat, sc, m_t, qn))
    dec_scr[...] = dec_all
    mrow_out[...] = mnew_all

    rowseg = lax.broadcasted_iota(jnp.int32, (L, MD), 0) // t_new

    def seq_body(j, carry):
        r0 = pl.multiple_of(j * t_new, t_new)
        dgrp = dec_scr[pl.ds(r0, t_new), :]
        for h in range(MH):
            c_prev = c_in[j, h]
            q_j = q_scr[h, pl.ds(r0, t_new), :].astype(BF16)
            inter_scr[h, pl.ds(r0, t_new), :] = _dot_nt(q_j, c_prev.astype(BF16))
            wv_j = jnp.where(rowseg == j, wv_scr[h], 0.0).astype(BF16)
            c_out[j, h] = dgrp[0:1, h:h + 1] * c_prev + _dot_tn(wv_j, k_scr[h])
        return carry

    lax.fori_loop(0, nseg, seq_body, 0)

    for h in range(MH):
        sl = slice(h * MD, (h + 1) * MD)
        dmat, sc, m_t, qn = head[h]
        out_ref[:, sl] = _mlstm_head_out(q_scr[h].astype(BF16), k_scr[h], vm_ref[:, sl], dmat, sc, m_t, inter_scr[h], qn,
                                         om_ref[:, sl], gmh_ref[:, sl])


def _mlstm_sample(qk, bufpad, vm, om, gates, mrows, nrows, c_state, n_state_t, wconv, gmh, *, t_new):
    L = MD
    r = qk.shape[0]
    nseg = L // t_new
    blk = lambda n: pl.BlockSpec((L, n), lambda i: (i, 0))
    return pl.pallas_call(
        functools.partial(_mlstm_sample_body, t_new=t_new),
        grid=(r // L,),
        in_specs=[blk(2 * MW), blk(2 * MW), blk(MW), blk(MW), blk(LANES), blk(LANES), blk(MW),
                  pl.BlockSpec((nseg, MH, MD, MD), lambda i: (i, 0, 0, 0)),
                  pl.BlockSpec((MH, nseg, MD), lambda i: (0, i, 0)),
                  _const_spec((QK_CONV, 2 * MW)), _const_spec((1, MW))],
        out_specs=[blk(MW), pl.BlockSpec((nseg, MH, MD, MD), lambda i: (i, 0, 0, 0)),
                   pl.BlockSpec((MH, nseg, MD), lambda i: (0, i, 0)), blk(LANES)],
        out_shape=[jax.ShapeDtypeStruct((r, MW), BF16), jax.ShapeDtypeStruct(c_state.shape, F32),
                   jax.ShapeDtypeStruct(n_state_t.shape, F32), jax.ShapeDtypeStruct((r, LANES), F32)],
        scratch_shapes=[pltpu.VMEM((MH, L, MD), F32), pltpu.VMEM((MH, L, MD), BF16), pltpu.VMEM((MH, L, MD), F32),
                        pltpu.VMEM((MH, L, MD), F32), pltpu.VMEM((L, LANES), F32)],
        compiler_params=_params(("arbitrary",)),
        name="mlstm_sample",
    )(qk, bufpad, vm, om, gates, mrows, nrows, c_state, n_state_t, wconv, gmh)


def _fox_prompt_body(qa_ref, ka_ref, va_ref, o_ref, m_scr, acc_scr, s0_scr, s1_scr, *, tq):
    i = pl.program_id(2)
    row = lax.broadcasted_iota(jnp.int32, (tq, tq), 0)
    colm = lax.broadcasted_iota(jnp.int32, (tq, tq), 1)
    nheads = m_scr.shape[0]
    m_scr[...] = jnp.full(m_scr.shape, NEG, F32)
    acc_scr[...] = jnp.zeros_like(acc_scr)

    def rows(j):
        return pl.ds(pl.multiple_of(j * tq, tq), tq)

    def produce(j, s_scr):
        for hh in range(nheads):
            sl = slice(hh * LANES, (hh + 1) * LANES)
            s_scr[hh] = _dot_nt(qa_ref[:, sl], ka_ref[rows(j), sl])

    def consume(j, s_scr, masked):
        for hh in range(nheads):
            sl = slice(hh * LANES, (hh + 1) * LANES)
            s = s_scr[hh]
            if masked:
                s = jnp.where(colm <= row, s, NEG)
            m_old = m_scr[hh, :, 0:1]
            m_new = jnp.maximum(m_old, jnp.max(s, axis=-1, keepdims=True))
            p = jnp.exp(s - m_new)
            acc_scr[hh] = jnp.exp(m_old - m_new) * acc_scr[hh] + _dot(p.astype(BF16), va_ref[rows(j), sl])
            m_scr[hh] = jnp.broadcast_to(m_new, (tq, LANES))

    produce(0, s0_scr)

    def pair_body(t, carry):
        produce(2 * t + 1, s1_scr)
        consume(2 * t, s0_scr, False)
        produce(2 * t + 2, s0_scr)
        consume(2 * t + 1, s1_scr, False)
        return carry

    lax.fori_loop(0, i // 2, pair_body, 0)

    @pl.when(i % 2 == 0)
    def _():
        consume(i, s0_scr, True)

    @pl.when(i % 2 == 1)
    def _():
        produce(i, s1_scr)
        consume(i - 1, s0_scr, False)
        consume(i, s1_scr, True)

    outs = []
    for hh in range(nheads):
        acc = acc_scr[hh]
        outs.append(acc[:, 0:FD] / acc[:, FD:FD + 1])
    o_ref[...] = jnp.concatenate(outs, axis=1)


def _fox_prompt(qa, ka, va, *, bsz, seq, tq):
    nq = seq // tq
    w2 = 2 * LANES
    return pl.pallas_call(
        functools.partial(_fox_prompt_body, tq=tq),
        grid=(bsz, FH // 2, nq),
        in_specs=[pl.BlockSpec((tq, w2), lambda b, hp, i: (b * nq + i, hp)),
                  pl.BlockSpec((seq, w2), lambda b, hp, i: (b, hp)),
                  pl.BlockSpec((seq, w2), lambda b, hp, i: (b, hp))],
        out_specs=pl.BlockSpec((tq, 2 * FD), lambda b, hp, i: (b * nq + i, hp)),
        out_shape=jax.ShapeDtypeStruct((bsz * seq, FW), F32),
        scratch_shapes=[pltpu.VMEM((2, tq, LANES), F32), pltpu.VMEM((2, tq, LANES), F32),
                        pltpu.VMEM((2, tq, tq), F32), pltpu.VMEM((2, tq, tq), F32)],
        compiler_params=_params(("arbitrary", "arbitrary", "arbitrary")),
        name="fox_prompt",
    )(qa, ka, va)


def _fox_sample_body(pt_ref, q_ref, kn_ref, vn_ref, cn_ref, rep_ref, msuf_ref, msel_ref, lf_hbm, k_hbm, v_hbm,
                     o_ref,
                     kbuf, vbuf, lfbuf, bias_scr, m_scr, l_scr, acc_scr, ksem, vsem, lfsem,
                     *, nseq, npg, ch, t_new):
    b = pl.program_id(0)
    nch = npg // ch
    nslot = kbuf.shape[0]
    nrow = FH * t_new

    def chunk_of(g):
        return g // nch, g % nch

    def kv_copies(seq, c, slot):
        cps = []
        for pi in range(ch):
            pg = pt_ref[seq, c * ch + pi]
            cps.append(pltpu.make_async_copy(k_hbm.at[pg], kbuf.at[slot, pi], ksem.at[slot]))
            cps.append(pltpu.make_async_copy(v_hbm.at[pg], vbuf.at[slot, pi], vsem.at[slot]))
        return cps

    def chunk_tile(buf, slot):
        return jnp.concatenate([buf[slot, pi].reshape(FW, LANES) for pi in range(ch)], axis=1).astype(BF16)

    def lf_copy(seq, p, slot):
        return pltpu.make_async_copy(lf_hbm.at[pt_ref[seq, p]], lfbuf.at[slot, p], lfsem.at[slot])

    def lf_start(seq, slot):
        def body(p, carry):
            lf_copy(seq, p, slot).start()
            return carry
        lax.fori_loop(0, npg, body, 0)

    def lf_wait(seq, slot):
        def body(p, carry):
            lf_copy(seq, p, slot).wait()
            return carry
        lax.fori_loop(0, npg, body, 0)

    @pl.when(b == 0)
    def _():
        lf_start(0, 0)
        for g0 in range(nslot - 1):
            for cp in kv_copies(g0 // nch, g0 % nch, g0):
                cp.start()

    lslot = b % 2
    lf_wait(b, lslot)

    @pl.when(b + 1 < nseq)
    def _():
        lf_start(b + 1, 1 - lslot)

    lf2 = lfbuf[lslot].reshape(npg * FH, LANES)
    suf_in = _dot3_lhs(lf2, msuf_ref[...])
    cross = jnp.sum(_dot3_rhs(msel_ref[...], lf2), axis=-1, keepdims=True)
    bias_scr[...] = (suf_in + cross).reshape(npg, FH, LANES)

    q = q_ref[0].astype(F32)
    rr = lax.broadcasted_iota(jnp.int32, (nrow, FW), 0) // t_new
    cc = lax.broadcasted_iota(jnp.int32, (nrow, FW), 1) // FD
    qbd = jnp.where(rr == cc, jnp.concatenate([q] * FH, axis=0), 0.0).astype(BF16)
    rep = rep_ref[...]

    def bias_rows(bias8):
        n = bias8.shape[1]
        a, b2, c2 = _split3(bias8)
        return jnp.concatenate([a.astype(F32), b2.astype(F32), c2.astype(F32),
                                jnp.zeros((LANES - 3 * FH, n), F32)], axis=0).astype(BF16)

    pad = jnp.zeros((LANES - t_new, FW), F32)
    kn = jnp.concatenate([kn_ref[0], pad], axis=0).astype(BF16)
    vn = jnp.concatenate([vn_ref[0], pad], axis=0).astype(BF16)
    s = _dot_nt(qbd, kn) + _dot(rep, bias_rows(-cn_ref[0]))
    qpos = lax.broadcasted_iota(jnp.int32, (nrow, LANES), 0) % t_new
    kpos = lax.broadcasted_iota(jnp.int32, (nrow, LANES), 1)
    s = jnp.where(kpos <= qpos, s, NEG)
    m0 = jnp.max(s, axis=-1, keepdims=True)
    p = jnp.exp(s - m0)
    m_scr[...] = jnp.broadcast_to(m0, m_scr.shape)
    l_scr[...] = jnp.broadcast_to(jnp.sum(p, axis=-1, keepdims=True), l_scr.shape)
    acc_scr[...] = _dot(p.astype(BF16), vn)

    def chunk_body(c, carry):
        g = b * nch + c
        slot = g % nslot
        for cp in kv_copies(b, c, slot):
            cp.wait()

        ahead = g + (nslot - 1)

        @pl.when(ahead < nseq * nch)
        def _():
            nb, ncx = chunk_of(ahead)
            for cp in kv_copies(nb, ncx, ahead % nslot):
                cp.start()

        kt = chunk_tile(kbuf, slot)
        blk = bias_scr[pl.ds(c * ch, ch)]
        bias8 = jnp.concatenate([blk[pi] for pi in range(ch)], axis=1)
        s = _dot(qbd, kt) + _dot(rep, bias_rows(bias8))
        m_old = m_scr[:, 0:1]
        m_new = jnp.maximum(m_old, jnp.max(s, axis=-1, keepdims=True))
        p = jnp.exp(s - m_new)
        alpha = jnp.exp(m_old - m_new)
        vt = chunk_tile(vbuf, slot)
        acc_scr[...] = alpha * acc_scr[...] + _dot_nt(p.astype(BF16), vt)
        l_scr[...] = alpha * l_scr[...] + jnp.sum(p, axis=-1, keepdims=True)
        m_scr[...] = jnp.broadcast_to(m_new, m_scr.shape)
        return carry

    lax.fori_loop(0, nch, chunk_body, 0)

    accn = acc_scr[...] / l_scr[:, 0:1]
    out = jnp.zeros((t_new, FW), F32)
    hcol = lax.broadcasted_iota(jnp.int32, (t_new, FW), 1) // FD
    for h in range(FH):
        out = jnp.where(hcol == h, accn[h * t_new:(h + 1) * t_new, :], out)
    o_ref[0] = out


def _fox_sample(page_table, q_s, k_new, v_new, c_new, rep, msuf, msel, lf_t, k_t, v_t, *, ch, nslot):
    nseq, t_new, _ = q_s.shape
    npg = page_table.shape[1]
    seqblk = lambda n: pl.BlockSpec((1, t_new, n), lambda b, pt: (b, 0, 0))
    cst = lambda shape: pl.BlockSpec(shape, lambda b, pt: (0,) * len(shape), pipeline_mode=pl.Buffered(1))
    anyspec = pl.BlockSpec(memory_space=pl.ANY)
    nrow = FH * t_new
    grid_spec = pltpu.PrefetchScalarGridSpec(
        num_scalar_prefetch=1,
        grid=(nseq,),
        in_specs=[seqblk(FW), seqblk(FW), seqblk(FW), pl.BlockSpec((1, FH, LANES), lambda b, pt: (b, 0, 0)),
                  cst((nrow, LANES)), cst((LANES, LANES)), cst((npg * FH, npg * FH)), anyspec, anyspec, anyspec],
        out_specs=seqblk(FW),
        scratch_shapes=[pltpu.VMEM((nslot, ch, FH, FD, LANES), F32), pltpu.VMEM((nslot, ch, FH, FD, LANES), F32),
                        pltpu.VMEM((2, npg, FH, LANES), F32), pltpu.VMEM((npg, FH, LANES), F32),
                        pltpu.VMEM((nrow, LANES), F32), pltpu.VMEM((nrow, LANES), F32), pltpu.VMEM((nrow, FW), F32),
                        pltpu.SemaphoreType.DMA((nslot,)), pltpu.SemaphoreType.DMA((nslot,)),
                        pltpu.SemaphoreType.DMA((2,))],
    )
    return pl.pallas_call(
        functools.partial(_fox_sample_body, nseq=nseq, npg=npg, ch=ch, t_new=t_new),
        grid_spec=grid_spec,
        out_shape=jax.ShapeDtypeStruct((nseq, t_new, FW), F32),
        compiler_params=_params(("arbitrary",)),
        name="fox_sample",
    )(page_table, q_s, k_new, v_new, c_new, rep, msuf, msel, lf_t, k_t, v_t)


def _ffn_body(x_ref, om_ref, af_ref, of_ref, buf_ref, wo_ref, gf_ref, wu_ref, wc_ref, bc_ref, wd_ref,
              y_ref, tail_ref, ext, *, tiles_per_seq, t_new):
    tm = x_ref.shape[0]
    if t_new is None:
        @pl.when(pl.program_id(0) % tiles_per_seq == 0)
        def _():
            ext[0:SUB, :] = jnp.zeros((SUB, D_FF), F32)

    mixin = jnp.concatenate([om_ref[...], (of_ref[...] * af_ref[...]).astype(BF16)], axis=1)
    h = x_ref[...] + _dot(mixin, wo_ref[...])
    hn = (h * lax.rsqrt(jnp.mean(h * h, axis=-1, keepdims=True) + EPS) * gf_ref[...]).astype(BF16)
    up_a = _dot(hn, wu_ref[:, 0:D_FF])
    up_g = _dot(hn, wu_ref[:, D_FF:2 * D_FF])
    if t_new is None:
        ext[SUB:SUB + tm, :] = up_a
        conv = wc_ref[0:1, :] * ext[pl.ds(SUB - 2, tm), :]
        for j in range(1, FFN_CONV):
            conv = conv + wc_ref[j:j + 1, :] * ext[pl.ds(SUB - 2 + j, tm), :]
        ext[0:SUB, :] = up_a[tm - SUB:tm, :]
        tail_ref[...] = up_a[tm - SUB:tm, :]
    else:
        nseg = tm // t_new
        cur = up_a.reshape(nseg, t_new, D_FF)
        buf = buf_ref[...].reshape(nseg, t_new, D_FF)
        tpos = lax.broadcasted_iota(jnp.int32, cur.shape, 1)
        w = lambda j: wc_ref[j:j + 1, :].reshape(1, 1, D_FF)
        conv = w(FFN_CONV - 1) * cur
        for d in range(1, FFN_CONV):
            shifted = jnp.where(tpos >= d, pltpu.roll(cur, d, axis=1), pltpu.roll(buf, d, axis=1))
            conv = conv + w(FFN_CONV - 1 - d) * shifted
        conv = conv.reshape(tm, D_FF)
        tail_ref[...] = up_a
    act = (_silu(conv + bc_ref[...]) * up_g).astype(BF16)
    y_ref[...] = h + _dot(act, wd_ref[...])


def _ffn(x2d, outm, af, of, bufpad, w_out, g_ffn, w_up, w_ffconv, b_ffconv, w_down, *, tm, tiles_per_seq, t_new):
    r = x2d.shape[0]
    row = lambda n: pl.BlockSpec((tm, n), lambda i: (i, 0))
    if t_new is None:
        tail_spec = pl.BlockSpec((SUB, D_FF), lambda i: (i, 0))
        tail_shape = jax.ShapeDtypeStruct((r // tm * SUB, D_FF), F32)
        buf_spec = pl.BlockSpec((SUB, D_FF), lambda i: (0, 0))
    else:
        tail_spec = row(D_FF)
        tail_shape = jax.ShapeDtypeStruct((r, D_FF), F32)
        buf_spec = row(D_FF)
    return pl.pallas_call(
        functools.partial(_ffn_body, tiles_per_seq=tiles_per_seq, t_new=t_new),
        grid=(r // tm,),
        in_specs=[row(D_MODEL), row(MW), row(FW), row(FW), buf_spec,
                  _const_spec((D_MODEL, D_MODEL)), _const_spec((1, D_MODEL)), _const_spec((D_MODEL, 2 * D_FF)),
                  _const_spec((FFN_CONV, D_FF)), _const_spec((1, D_FF)), _const_spec((D_FF, D_MODEL))],
        out_specs=[row(D_MODEL), tail_spec],
        out_shape=[jax.ShapeDtypeStruct((r, D_MODEL), F32), tail_shape],
        scratch_shapes=[pltpu.VMEM((SUB + tm, D_FF), F32)],
        compiler_params=_params(("arbitrary",)),
        name="ffn",
    )(x2d, outm, af, of, bufpad, w_out, g_ffn, w_up, w_ffconv, b_ffconv, w_down)


def _lane_pad(v, n=LANES):
    return jnp.pad(v, ((0, 0), (0, n - v.shape[1])))


def _state_rows(state, t_new):
    nseq, k1, c = state.shape
    return jnp.pad(state, ((0, 0), (t_new - k1, 0), (0, 0))).reshape(nseq * t_new, c)


def kernel(x_prompt, x_sample, cache_k, cache_v, cache_logf, page_table, state_C, state_n, state_m, state_qkconv,
           state_ffnconv, g_mix, w_in, b_gate, w_qkconv, g_qf, g_kf, g_mh, w_out, g_ffn, w_up, w_ffconv, b_ffconv,
           w_down):
    assert w_in.shape[0] == 1, "single-layer kernel"
    bsz, seq, _ = x_prompt.shape
    dbsz, t_new, _ = x_sample.shape
    tm = 256
    tm_ffn = 512

    w = w_in[0]
    w_pad = jnp.concatenate([w[:, 0:C_G], _lane_pad(w[:, C_G:])], axis=1).astype(BF16)
    bg_pad = _lane_pad(b_gate.astype(F32))
    gq2 = jnp.tile(g_qf.astype(F32), (1, LANES // FD))
    gk2 = jnp.tile(g_kf.astype(F32), (1, LANES // FD))
    gmix = g_mix.astype(F32)
    wconv = w_qkconv[0].astype(F32)
    gmh = g_mh.astype(F32)
    wo_b, wu_b, wd_b = w_out[0].astype(BF16), w_up[0].astype(BF16), w_down[0].astype(BF16)
    gffn, wfc, bfc = g_ffn.astype(F32), w_ffconv[0].astype(F32), b_ffconv.astype(F32)

    ridx = jnp.arange(tm)
    tri_p = (ridx[None, :] <= ridx[:, None]).astype(BF16)
    tri_s = ((ridx[None, :] <= ridx[:, None]) & (ridx[None, :] // t_new == ridx[:, None] // t_new)).astype(BF16)

    xp = x_prompt.reshape(bsz * seq, D_MODEL).astype(F32)
    (qk_p, vm_p, om_p, qa_p, ka_p, va_p, of_p, g_p, _, kt_p, vt_p) = _proj(
        xp, gmix, w_pad, bg_pad, gq2, gk2, tri_p, tm=tm, tiles_per_seq=seq // tm, kv_transposed=True)
    outm_p, c_p, n_p, m_p = _mlstm_prompt(qk_p, vm_p, om_p, g_p, wconv, gmh, bsz=bsz, seq=seq)
    af_p = _fox_prompt(qa_p, ka_p, va_p, bsz=bsz, seq=seq, tq=512)
    y_p, tail_p = _ffn(xp, outm_p, af_p, of_p, jnp.zeros((SUB, D_FF), F32), wo_b, gffn, wu_b, wfc, bfc, wd_b,
                       tm=tm_ffn, tiles_per_seq=seq // tm_ffn, t_new=None)

    xs = x_sample.reshape(dbsz * t_new, D_MODEL).astype(F32)
    (qk_s, vm_s, om_s, qa_s, _, _, of_s, g_s, fc_s, kf_s, vf_s) = _proj(
        xs, gmix, w_pad, bg_pad, gq2, gk2, tri_s, tm=tm, tiles_per_seq=1, kv_transposed=False)
    mrows = _lane_pad(jnp.repeat(state_m[0].astype(F32), t_new, axis=0))
    nrows = jnp.repeat(state_n[0].astype(F32).reshape(dbsz, MW), t_new, axis=0)
    n_t = jnp.transpose(state_n[0].astype(F32), (1, 0, 2))
    outm_s, c_s, n_s_t, mrow_s = _mlstm_sample(
        qk_s, _state_rows(state_qkconv[0].astype(F32), t_new), vm_s, om_s, g_s, mrows, nrows,
        state_C[0].astype(F32), n_t, wconv, gmh, t_new=t_new)

    k_new = kf_s.reshape(dbsz, t_new, FH, FD)
    v_new = vf_s.reshape(dbsz, t_new, FH, FD)
    q_s = qa_s.reshape(dbsz, t_new, FH, LANES)[..., :FD].reshape(dbsz, t_new, FW)
    c_new = _lane_pad(jnp.transpose(fc_s.reshape(dbsz, t_new, LANES)[:, :, G_FF:G_FF + FH], (0, 2, 1))
                      .reshape(dbsz * FH, t_new)).reshape(dbsz, FH, LANES)
    npg = page_table.shape[1]
    nrow = FH * t_new
    rr = jnp.arange(nrow)[:, None] // t_new
    cc = jnp.arange(LANES)[None, :]
    rep = ((cc < 3 * FH) & (cc % FH == rr)).astype(BF16)
    kk = jnp.arange(LANES)
    msuf = (kk[:, None] > kk[None, :]).astype(BF16)
    pr = jnp.arange(npg * FH)
    msel = ((pr[None, :] % FH == pr[:, None] % FH) & (pr[None, :] // FH > pr[:, None] // FH)).astype(BF16)
    k_t = jnp.transpose(cache_k[0], (0, 2, 3, 1)).astype(F32)
    v_t = jnp.transpose(cache_v[0], (0, 2, 3, 1)).astype(F32)
    lf_t = jnp.transpose(cache_logf[0], (0, 2, 1)).astype(F32)
    fs_args = (page_table.astype(jnp.int32), q_s, k_new.reshape(dbsz, t_new, FW),
               v_new.reshape(dbsz, t_new, FW), c_new, rep, msuf, msel, lf_t, k_t, v_t)
    af_s = _fox_sample(*fs_args, ch=16, nslot=3)
    y_s, tail_s = _ffn(xs, outm_s, af_s.reshape(dbsz * t_new, FW), of_s,
                       _state_rows(state_ffnconv[0].astype(F32), t_new), wo_b, gffn, wu_b, wfc, bfc, wd_b,
                       tm=tm, tiles_per_seq=1, t_new=t_new)

    dt = x_prompt.dtype
    y_prompt = y_p.reshape(bsz, seq, D_MODEL).astype(dt)
    y_sample = y_s.reshape(dbsz, t_new, D_MODEL).astype(x_sample.dtype)
    k_prompt = jnp.transpose(kt_p, (0, 3, 1, 2))[None]
    v_prompt = jnp.transpose(vt_p, (0, 3, 1, 2))[None]
    logf_prompt = g_p.reshape(bsz, seq, LANES)[:, :, G_FF:G_FF + FH][None]
    c_prompt = c_p[None]
    n_prompt = n_p[:, :MH, :][None]
    m_prompt = m_p[:, :MH, 0][None]
    qkconv_prompt = qk_p.reshape(bsz, seq, 2 * MW)[:, seq - (QK_CONV - 1):, :][None]
    ffnconv_prompt = tail_p.reshape(bsz, seq // tm_ffn, SUB, D_FF)[:, -1, SUB - (FFN_CONV - 1):, :][None]
    k_sample = k_new[None]
    v_sample = v_new[None]
    logf_sample = g_s.reshape(dbsz, t_new, LANES)[:, :, G_FF:G_FF + FH][None]
    c_sample = c_s[None]
    n_sample = jnp.transpose(n_s_t, (1, 0, 2))[None]
    m_sample = mrow_s.reshape(dbsz, t_new, LANES)[:, t_new - 1, :MH][None]
    qkconv_sample = qk_s.reshape(dbsz, t_new, 2 * MW)[:, t_new - (QK_CONV - 1):, :][None]
    ffnconv_sample = tail_s.reshape(dbsz, t_new, D_FF)[:, t_new - (FFN_CONV - 1):, :][None]
    return (y_prompt, y_sample, k_prompt, v_prompt, logf_prompt, c_prompt, n_prompt, m_prompt, qkconv_prompt,
            ffnconv_prompt, k_sample, v_sample, logf_sample, c_sample, n_sample, m_sample, qkconv_sample,
            ffnconv_sample)
```

```python
import functools

import jax
import jax.numpy as jnp
from jax import lax
from jax.experimental import pallas as pl
from jax.experimental.pallas import tpu as pltpu

F32, BF16 = jnp.float32, jnp.bfloat16

D_MODEL = 1024
MH, MD = 4, 128
MW = MH * MD
FH, FD = 8, 64
FW = FH * FD
QK_CONV, FFN_CONV = 4, 3
D_FF = 2816
N_GATES = 2 * MH + FH
EPS = 1e-6
NEG = -1e30
LANES = 128
SUB = 8
VMEM_LIMIT = 56 * 1024 * 1024

C_QK, C_VM, C_OM = 0, 2 * MW, 3 * MW
C_QF = 4 * MW
C_KF, C_VF, C_OF, C_G = C_QF + FW, C_QF + 2 * FW, C_QF + 3 * FW, C_QF + 4 * FW
G_FM, G_FF = MH, 2 * MH


def _split3(x):
    a = x.astype(BF16)
    r = x - a.astype(F32)
    b = r.astype(BF16)
    c = (r - b.astype(F32)).astype(BF16)
    return a, b, c


def _dot(a, b):
    return jnp.dot(a, b, preferred_element_type=F32)


def _dot_nt(a, b):
    return lax.dot_general(a, b, (((1,), (1,)), ((), ())), preferred_element_type=F32)


def _dot_tn(a, b):
    return lax.dot_general(a, b, (((0,), (0,)), ((), ())), preferred_element_type=F32)


def _dot3_rhs(m, x):
    a, b, c = _split3(x)
    return _dot(m, c) + _dot(m, b) + _dot(m, a)


def _dot3_lhs(x, m):
    a, b, c = _split3(x)
    return _dot(c, m) + _dot(b, m) + _dot(a, m)


def _log_sigmoid(x):
    return jnp.minimum(x, 0.0) - jnp.log1p(jnp.exp(-jnp.abs(x)))


def _silu(x):
    return x * jax.nn.sigmoid(x)


def _const_spec(shape):
    n = len(shape)
    return pl.BlockSpec(shape, lambda *_: (0,) * n, pipeline_mode=pl.Buffered(1))


def _params(sem):
    return pltpu.CompilerParams(dimension_semantics=sem, vmem_limit_bytes=VMEM_LIMIT)


def _proj_body(x_ref, gmix_ref, w_ref, wg_ref, bg_ref, gq_ref, gk_ref, tri_ref,
               qk_ref, vm_ref, om_ref, qa_ref, ka_ref, va_ref, of_ref, g_ref, fc_ref, kf_ref, vf_ref,
               carry_ref, *, tiles_per_seq, kv_transposed):
    tm = x_ref.shape[0]
    if tiles_per_seq > 1:
        @pl.when(pl.program_id(0) % tiles_per_seq == 0)
        def _():
            carry_ref[...] = jnp.zeros_like(carry_ref)

    x = x_ref[...]
    ms = jnp.mean(x * x, axis=-1, keepdims=True)
    xn = (x * lax.rsqrt(ms + EPS) * gmix_ref[...]).astype(BF16)

    def proj(lo, n):
        return _dot(xn, w_ref[:, lo:lo + n])

    g = _dot(xn, wg_ref[...]) + bg_ref[...]
    col = lax.broadcasted_iota(jnp.int32, g.shape, 1)
    gates = jnp.where(col < MH, g, jnp.where(col < N_GATES, _log_sigmoid(g), 0.0))
    g_ref[...] = gates

    cum = _dot3_rhs(tri_ref[...], gates)
    if tiles_per_seq > 1:
        cum = cum + carry_ref[...]
        carry_ref[...] = cum[tm - 1:tm, :]
    fc_ref[...] = cum

    nf = -cum
    n1 = nf.astype(BF16).astype(F32)
    r1 = nf - n1
    n2 = r1.astype(BF16).astype(F32)
    n3 = (r1 - n2).astype(BF16).astype(F32)

    pq = proj(C_QF, FW)
    pk = proj(C_KF, FW)
    pv = proj(C_VF, FW)
    qk_ref[...] = proj(C_QK, 2 * MW)
    vm_ref[...] = proj(C_VM, MW).astype(BF16)
    om_ref[...] = jax.nn.sigmoid(proj(C_OM, MW))
    of_ref[...] = jax.nn.sigmoid(proj(C_OF, FW))
    lane = lax.broadcasted_iota(jnp.int32, (tm, LANES), 1)
    low = lane < FD
    gq = gq_ref[...]
    gk = gk_ref[...]

    def pair_norm(x, g2):
        sq = x * x
        ms_lo = jnp.sum(jnp.where(low, sq, 0.0), axis=-1, keepdims=True) * (1.0 / FD)
        ms_hi = jnp.sum(jnp.where(low, 0.0, sq), axis=-1, keepdims=True) * (1.0 / FD)
        return x * jnp.where(low, lax.rsqrt(ms_lo + EPS), lax.rsqrt(ms_hi + EPS)) * g2

    def store_f32(ref, j, pair):
        if kv_transposed:
            tr = jnp.concatenate([pair[r0:r0 + LANES, :].T for r0 in range(0, tm, LANES)], axis=1)
            ref[0, 2 * j] = tr[0:FD, :]
            ref[0, 2 * j + 1] = tr[FD:2 * FD, :]
        else:
            ref[:, j * LANES:(j + 1) * LANES] = pair

    for j in range(FH // 2):
        psl = slice(j * LANES, (j + 1) * LANES)
        qn2 = pair_norm(pq[:, psl], gq) * (FD ** -0.5)
        kn2 = pair_norm(pk[:, psl], gk)
        xv2 = pv[:, psl]
        store_f32(kf_ref, j, kn2)
        store_f32(vf_ref, j, xv2)
        for odd in range(2):
            h = 2 * j + odd
            sl = slice(h * LANES, (h + 1) * LANES)
            qn, kn, xv = ((pltpu.roll(a, FD, axis=1) if odd else a) for a in (qn2, kn2, xv2))
            c = G_FF + h
            b1 = jnp.broadcast_to(n1[:, c:c + 1], (tm, LANES))
            b2 = jnp.broadcast_to(n2[:, c:c + 1], (tm, LANES))
            b3 = jnp.broadcast_to(n3[:, c:c + 1], (tm, LANES))
            zero = jnp.zeros((tm, LANES), F32)
            qa = jnp.where(low, qn, jnp.where(lane < FD + 3, 1.0, zero))
            ka = jnp.where(low, kn, jnp.where(lane == FD, b1, jnp.where(lane == FD + 1, b2,
                                                                       jnp.where(lane == FD + 2, b3, zero))))
            va = jnp.where(low, xv, jnp.where(lane == FD, 1.0, zero))
            qa_ref[:, sl] = qa.astype(BF16)
            ka_ref[:, sl] = ka.astype(BF16)
            va_ref[:, sl] = va.astype(BF16)


def _proj(x2d, gmix, w_main, w_gate, bg_pad, gq2, gk2, tri, *, tm, tiles_per_seq, kv_transposed):
    r = x2d.shape[0]
    row = lambda n: pl.BlockSpec((tm, n), lambda i: (i, 0))
    widths = [(2 * MW, F32), (MW, BF16), (MW, F32), (FH * LANES, BF16), (FH * LANES, BF16), (FH * LANES, BF16),
              (FW, F32), (LANES, F32), (LANES, F32)]
    out_specs = [row(n) for n, _ in widths]
    out_shape = [jax.ShapeDtypeStruct((r, n), dt) for n, dt in widths]
    if kv_transposed:
        nseq = r // (tm * tiles_per_seq)
        kv_spec = pl.BlockSpec((1, FH, FD, tm), lambda i: (i // tiles_per_seq, 0, 0, i % tiles_per_seq))
        kv_shape = jax.ShapeDtypeStruct((nseq, FH, FD, tm * tiles_per_seq), F32)
    else:
        kv_spec, kv_shape = row(FW), jax.ShapeDtypeStruct((r, FW), F32)
    return pl.pallas_call(
        functools.partial(_proj_body, tiles_per_seq=tiles_per_seq, kv_transposed=kv_transposed),
        grid=(r // tm,),
        in_specs=[row(D_MODEL), _const_spec((1, D_MODEL)), _const_spec((D_MODEL, C_G)), _const_spec((D_MODEL, LANES)),
                  _const_spec((1, LANES)), _const_spec((1, LANES)), _const_spec((1, LANES)), _const_spec((tm, tm))],
        out_specs=out_specs + [kv_spec, kv_spec],
        out_shape=out_shape + [kv_shape, kv_shape],
        scratch_shapes=[pltpu.VMEM((1, LANES), F32)],
        compiler_params=_params(("arbitrary",)),
        name="proj",
    )(x2d, gmix, w_main, w_gate, bg_pad, gq2, gk2, tri)


def _mlstm_gate_math(h, gates, g_t, cum, cum_t, tot, tot_t, mp_col, cmask, smask):
    i_col, i_row = gates[:, h:h + 1], g_t[h:h + 1, :]
    c = G_FM + h
    b_col, b_row = cum[:, c:c + 1], cum_t[c:c + 1, :]
    t_col, t_row = tot[:, c:c + 1], tot_t[c:c + 1, :]
    logd = jnp.where(cmask, b_col + (i_row - b_row), NEG)
    m_t = jnp.maximum(b_col + mp_col, jnp.max(logd, axis=-1, keepdims=True))
    z = jnp.where(smask, (t_row - b_row) + i_row, NEG)
    m_new = jnp.maximum(t_col + mp_col, jnp.max(z, axis=-1, keepdims=True))
    dmat = jnp.exp(logd - m_t)
    sc = jnp.exp(b_col + mp_col - m_t)
    w_col = jnp.exp(t_col - b_col + i_col - m_new)
    decay = jnp.exp(t_col + mp_col - m_new)
    return dmat, sc, m_t, m_new, w_col, decay


def _mlstm_head_out(q_bf, k_bf, v_bf, dmat, sc, m_t, inter, qn, om, gmh):
    s = _dot_nt(q_bf, k_bf) * dmat
    num = _dot(s.astype(BF16), v_bf) + sc * inter
    den = jnp.sum(s, axis=-1, keepdims=True) + sc * qn
    hc = num / jnp.maximum(jnp.abs(den), jnp.exp(-m_t))
    hn = hc * lax.rsqrt(jnp.mean(hc * hc, axis=-1, keepdims=True) + EPS) * gmh
    return (om * hn).astype(BF16)


def _mlstm_prompt_body(qk_ref, halo_ref, vm_ref, om_ref, g_ref, wc_ref, gmh_ref,
                       out_ref, c_out, n_out, m_out,
                       ext, c_scr, n_scr, m_scr):
    c_idx = pl.program_id(1)
    L = qk_ref.shape[0]

    @pl.when(c_idx == 0)
    def _():
        c_scr[...] = jnp.zeros_like(c_scr)
        n_scr[...] = jnp.zeros_like(n_scr)
        m_scr[...] = jnp.zeros_like(m_scr)

    ext[0:SUB, :] = jnp.where(c_idx == 0, 0.0, halo_ref[...])
    ext[SUB:SUB + L, :] = qk_ref[...]
    conv = wc_ref[0:1, :] * ext[pl.ds(SUB - 3, L), :]
    for j in range(1, QK_CONV):
        conv = conv + wc_ref[j:j + 1, :] * ext[pl.ds(SUB - 3 + j, L), :]
    qk = _silu(conv)

    row = lax.broadcasted_iota(jnp.int32, (L, L), 0)
    colm = lax.broadcasted_iota(jnp.int32, (L, L), 1)
    cmask = colm <= row
    smask = colm >= 0
    tri = jnp.where(cmask, 1.0, 0.0).astype(BF16)
    ones = jnp.ones((L, L), BF16)

    gates = g_ref[...]
    cum = _dot3_rhs(tri, gates)
    tot = _dot3_rhs(ones, gates)
    g_t, cum_t, tot_t = gates.T, cum.T, tot.T

    for h in range(MH):
        sl = slice(h * MD, (h + 1) * MD)
        q = qk[:, sl]
        k = qk[:, MW + h * MD:MW + (h + 1) * MD] * (MD ** -0.5)
        q_bf, k_bf = q.astype(BF16), k.astype(BF16)
        v_bf = vm_ref[:, sl]
        mp = m_scr[h:h + 1, 0:1]
        dmat, sc, m_t, m_new, w_col, decay = _mlstm_gate_math(h, gates, g_t, cum, cum_t, tot, tot_t, mp, cmask, smask)
        c_prev = c_scr[h]
        n_prev = n_scr[h:h + 1, :]
        inter = _dot_nt(q_bf, c_prev.astype(BF16))
        qn = jnp.sum(q * n_prev, axis=-1, keepdims=True)
        out_ref[:, sl] = _mlstm_head_out(q_bf, k_bf, v_bf, dmat, sc, m_t, inter, qn, om_ref[:, sl], gmh_ref[:, sl])
        dec = decay[L - 1:L, :]
        wv = (w_col * v_bf.astype(F32)).astype(BF16)
        c_scr[h] = dec * c_prev + _dot_tn(wv, k_bf)
        n_scr[h:h + 1, :] = dec * n_prev + jnp.sum(w_col * k, axis=0, keepdims=True)
        m_scr[h:h + 1, :] = jnp.broadcast_to(m_new[L - 1:L, :], (1, LANES))

    @pl.when(c_idx == pl.num_programs(1) - 1)
    def _():
        c_out[0] = c_scr[...]
        n_out[0] = n_scr[...]
        m_out[0] = m_scr[...]


def _mlstm_prompt(qk, vm, om, gates, wconv, gmh, *, bsz, seq):
    L = MD
    nck = seq // L
    blk = lambda n: pl.BlockSpec((L, n), lambda b, c: (b * nck + c, 0))
    halo = pl.BlockSpec((SUB, 2 * MW), lambda b, c: (jnp.maximum((b * nck + c) * (L // SUB) - 1, 0), 0))
    st = lambda shape: pl.BlockSpec((1,) + shape, lambda b, c: (b,) + (0,) * len(shape))
    return pl.pallas_call(
        _mlstm_prompt_body,
        grid=(bsz, nck),
        in_specs=[blk(2 * MW), halo, blk(MW), blk(MW), blk(LANES), _const_spec((QK_CONV, 2 * MW)), _const_spec((1, MW))],
        out_specs=[blk(MW), st((MH, MD, MD)), st((SUB, MD)), st((SUB, LANES))],
        out_shape=[jax.ShapeDtypeStruct((bsz * seq, MW), BF16),
                   jax.ShapeDtypeStruct((bsz, MH, MD, MD), F32),
                   jax.ShapeDtypeStruct((bsz, SUB, MD), F32),
                   jax.ShapeDtypeStruct((bsz, SUB, LANES), F32)],
        scratch_shapes=[pltpu.VMEM((SUB + L, 2 * MW), F32), pltpu.VMEM((MH, MD, MD), F32),
                        pltpu.VMEM((SUB, MD), F32), pltpu.VMEM((SUB, LANES), F32)],
        compiler_params=_params(("arbitrary", "arbitrary")),
        name="mlstm_prompt",
    )(qk, qk, vm, om, gates, wconv, gmh)


def _mlstm_sample_body(qk_ref, buf_ref, vm_ref, om_ref, g_ref, mrow_ref, nrow_ref, c_in, n_in, wc_ref, gmh_ref,
                       out_ref, c_out, n_out, mrow_out,
                       q_scr, k_scr, wv_scr, inter_scr, dec_scr, *, t_new):
    L = qk_ref.shape[0]
    nseg = L // t_new
    cur = qk_ref[...].reshape(nseg, t_new, 2 * MW)
    buf = buf_ref[...].reshape(nseg, t_new, 2 * MW)
    tpos = lax.broadcasted_iota(jnp.int32, cur.shape, 1)
    w = lambda j: wc_ref[j:j + 1, :].reshape(1, 1, 2 * MW)
    conv = w(QK_CONV - 1) * cur
    for d in range(1, QK_CONV):
        shifted = jnp.where(tpos >= d, pltpu.roll(cur, d, axis=1), pltpu.roll(buf, d, axis=1))
        conv = conv + w(QK_CONV - 1 - d) * shifted
    qk = _silu(conv).reshape(L, 2 * MW)

    row = lax.broadcasted_iota(jnp.int32, (L, L), 0)
    colm = lax.broadcasted_iota(jnp.int32, (L, L), 1)
    smask = (row // t_new) == (colm // t_new)
    cmask = smask & (colm <= row)
    tri = jnp.where(cmask, 1.0, 0.0).astype(BF16)
    ones = jnp.where(smask, 1.0, 0.0).astype(BF16)
    seg16 = jnp.where(lax.broadcasted_iota(jnp.int32, (nseg, L), 0)
                      == lax.broadcasted_iota(jnp.int32, (nseg, L), 1) // t_new, 1.0, 0.0).astype(BF16)

    gates = g_ref[...]
    cum = _dot3_rhs(tri, gates)
    tot = _dot3_rhs(ones, gates)
    g_t, cum_t, tot_t = gates.T, cum.T, tot.T
    lane = lax.broadcasted_iota(jnp.int32, (L, LANES), 1)

    head = []
    dec_all = jnp.zeros((L, LANES), F32)
    mnew_all = jnp.zeros((L, LANES), F32)
    for h in range(MH):
        sl = slice(h * MD, (h + 1) * MD)
        q = qk[:, sl]
        k = qk[:, MW + h * MD:MW + (h + 1) * MD] * (MD ** -0.5)
        v_bf = vm_ref[:, sl]
        mp = mrow_ref[:, h:h + 1]
        dmat, sc, m_t, m_new, w_col, decay = _mlstm_gate_math(h, gates, g_t, cum, cum_t, tot, tot_t, mp, cmask, smask)
        q_scr[h] = q
        k_scr[h] = k.astype(BF16)
        wv_scr[h] = w_col * v_bf.astype(F32)
        dec_all = jnp.where(lane == h, jnp.broadcast_to(decay, (L, LANES)), dec_all)
        mnew_all = jnp.where(lane == h, jnp.broadcast_to(m_new, (L, LANES)), mnew_all)
        n_add = _dot3_rhs(seg16, w_col * k)
        dec_seg = _dot3_rhs(seg16, decay * (1.0 / t_new) * jnp.ones((L, MD), F32))
        n_out[h] = dec_seg * n_in[h] + n_add
        qn = jnp.sum(q * nrow_ref[:, sl], axis=-1, keepdims=True)
        head.append((dmat, sc, m_t, qn))
    dec_scr[...] = dec_all
    mrow_out[...] = mnew_all

    rowseg = lax.broadcasted_iota(jnp.int32, (L, MD), 0) // t_new

    def seq_body(j, carry):
        r0 = pl.multiple_of(j * t_new, t_new)
        dgrp = dec_scr[pl.ds(r0, t_new), :]
        for h in range(MH):
            c_prev = c_in[j, h]
            q_j = q_scr[h, pl.ds(r0, t_new), :].astype(BF16)
            inter_scr[h, pl.ds(r0, t_new), :] = _dot_nt(q_j, c_prev.astype(BF16))
            wv_j = jnp.where(rowseg == j, wv_scr[h], 0.0).astype(BF16)
            c_out[j, h] = dgrp[0:1, h:h + 1] * c_prev + _dot_tn(wv_j, k_scr[h])
        return carry

    lax.fori_loop(0, nseg, seq_body, 0)

    for h in range(MH):
        sl = slice(h * MD, (h + 1) * MD)
        dmat, sc, m_t, qn = head[h]
        out_ref[:, sl] = _mlstm_head_out(q_scr[h].astype(BF16), k_scr[h], vm_ref[:, sl], dmat, sc, m_t, inter_scr[h], qn,
                                         om_ref[:, sl], gmh_ref[:, sl])


def _mlstm_sample(qk, bufpad, vm, om, gates, mrows, nrows, c_state, n_state_t, wconv, gmh, *, t_new):
    L = MD
    r = qk.shape[0]
    nseg = L // t_new
    blk = lambda n: pl.BlockSpec((L, n), lambda i: (i, 0))
    return pl.pallas_call(
        functools.partial(_mlstm_sample_body, t_new=t_new),
        grid=(r // L,),
        in_specs=[blk(2 * MW), blk(2 * MW), blk(MW), blk(MW), blk(LANES), blk(LANES), blk(MW),
                  pl.BlockSpec((nseg, MH, MD, MD), lambda i: (i, 0, 0, 0)),
                  pl.BlockSpec((MH, nseg, MD), lambda i: (0, i, 0)),
                  _const_spec((QK_CONV, 2 * MW)), _const_spec((1, MW))],
        out_specs=[blk(MW), pl.BlockSpec((nseg, MH, MD, MD), lambda i: (i, 0, 0, 0)),
                   pl.BlockSpec((MH, nseg, MD), lambda i: (0, i, 0)), blk(LANES)],
        out_shape=[jax.ShapeDtypeStruct((r, MW), BF16), jax.ShapeDtypeStruct(c_state.shape, F32),
                   jax.ShapeDtypeStruct(n_state_t.shape, F32), jax.ShapeDtypeStruct((r, LANES), F32)],
        scratch_shapes=[pltpu.VMEM((MH, L, MD), F32), pltpu.VMEM((MH, L, MD), BF16), pltpu.VMEM((MH, L, MD), F32),
                        pltpu.VMEM((MH, L, MD), F32), pltpu.VMEM((L, LANES), F32)],
        compiler_params=_params(("arbitrary",)),
        name="mlstm_sample",
    )(qk, bufpad, vm, om, gates, mrows, nrows, c_state, n_state_t, wconv, gmh)


def _fox_prompt_body(qa_ref, ka_ref, va_ref, o_ref, m_scr, acc_scr, s0_scr, s1_scr, *, tq):
    i = pl.program_id(2)
    row = lax.broadcasted_iota(jnp.int32, (tq, tq), 0)
    colm = lax.broadcasted_iota(jnp.int32, (tq, tq), 1)
    nheads = m_scr.shape[0]
    m_scr[...] = jnp.full(m_scr.shape, NEG, F32)
    acc_scr[...] = jnp.zeros_like(acc_scr)

    def rows(j):
        return pl.ds(pl.multiple_of(j * tq, tq), tq)

    def produce(j, s_scr):
        for hh in range(nheads):
            sl = slice(hh * LANES, (hh + 1) * LANES)
            s_scr[hh] = _dot_nt(qa_ref[:, sl], ka_ref[rows(j), sl])

    def consume(j, s_scr, masked):
        for hh in range(nheads):
            sl = slice(hh * LANES, (hh + 1) * LANES)
            s = s_scr[hh]
            if masked:
                s = jnp.where(colm <= row, s, NEG)
            m_old = m_scr[hh]
            m_new = jnp.maximum(m_old, jnp.max(s, axis=-1, keepdims=True))
            p = jnp.exp(s - jnp.concatenate([m_new] * (tq // LANES), axis=1))
            acc_scr[hh] = jnp.exp(m_old - m_new) * acc_scr[hh] + _dot(p.astype(BF16), va_ref[rows(j), sl])
            m_scr[hh] = m_new

    produce(0, s0_scr)

    def pair_body(t, carry):
        produce(2 * t + 1, s1_scr)
        consume(2 * t, s0_scr, False)
        produce(2 * t + 2, s0_scr)
        consume(2 * t + 1, s1_scr, False)
        return carry

    lax.fori_loop(0, i // 2, pair_body, 0)

    @pl.when(i % 2 == 0)
    def _():
        consume(i, s0_scr, True)

    @pl.when(i % 2 == 1)
    def _():
        produce(i, s1_scr)
        consume(i - 1, s0_scr, False)
        consume(i, s1_scr, True)

    outs = []
    for hh in range(nheads):
        acc = acc_scr[hh]
        outs.append(acc[:, 0:FD] / acc[:, FD:FD + 1])
    o_ref[...] = jnp.concatenate(outs, axis=1)


def _fox_prompt(qa, ka, va, *, bsz, seq, tq):
    nq = seq // tq
    w2 = 2 * LANES
    return pl.pallas_call(
        functools.partial(_fox_prompt_body, tq=tq),
        grid=(bsz, FH // 2, nq),
        in_specs=[pl.BlockSpec((tq, w2), lambda b, hp, i: (b * nq + i, hp)),
                  pl.BlockSpec((seq, w2), lambda b, hp, i: (b, hp)),
                  pl.BlockSpec((seq, w2), lambda b, hp, i: (b, hp))],
        out_specs=pl.BlockSpec((tq, 2 * FD), lambda b, hp, i: (b * nq + i, hp)),
        out_shape=jax.ShapeDtypeStruct((bsz * seq, FW), F32),
        scratch_shapes=[pltpu.VMEM((2, tq, LANES), F32), pltpu.VMEM((2, tq, LANES), F32),
                        pltpu.VMEM((2, tq, tq), F32), pltpu.VMEM((2, tq, tq), F32)],
        compiler_params=_params(("arbitrary", "arbitrary", "arbitrary")),
        name="fox_prompt",
    )(qa, ka, va)


def _fox_sample_body(pt_ref, q_ref, kn_ref, vn_ref, cn_ref, rep_ref, msuf_ref, msel_ref, lf_hbm, k_hbm, v_hbm,
                     o_ref,
                     kbuf, vbuf, lfbuf, bias_scr, m_scr, l_scr, acc_scr, ksem, vsem, lfsem,
                     *, nseq, npg, ch, t_new):
    b = pl.program_id(0)
    nch = npg // ch
    nslot = kbuf.shape[0]
    nrow = FH * t_new

    def chunk_of(g):
        return g // nch, g % nch

    def kv_copies(seq, c, slot):
        cps = []
        for pi in range(ch):
            pg = pt_ref[seq, c * ch + pi]
            cps.append(pltpu.make_async_copy(k_hbm.at[pg], kbuf.at[slot, pi], ksem.at[slot]))
            cps.append(pltpu.make_async_copy(v_hbm.at[pg], vbuf.at[slot, pi], vsem.at[slot]))
        return cps

    def chunk_tile(buf, slot):
        return jnp.concatenate([buf[slot, pi].reshape(FW, LANES) for pi in range(ch)], axis=1).astype(BF16)

    def lf_copy(seq, p, slot):
        return pltpu.make_async_copy(lf_hbm.at[pt_ref[seq, p]], lfbuf.at[slot, p], lfsem.at[slot])

    def lf_start(seq, slot):
        def body(p, carry):
            lf_copy(seq, p, slot).start()
            return carry
        lax.fori_loop(0, npg, body, 0)

    def lf_wait(seq, slot):
        def body(p, carry):
            lf_copy(seq, p, slot).wait()
            return carry
        lax.fori_loop(0, npg, body, 0)

    @pl.when(b == 0)
    def _():
        lf_start(0, 0)
        for g0 in range(nslot - 1):
            for cp in kv_copies(g0 // nch, g0 % nch, g0):
                cp.start()

    lslot = b % 2
    lf_wait(b, lslot)

    @pl.when(b + 1 < nseq)
    def _():
        lf_start(b + 1, 1 - lslot)

    lf2 = lfbuf[lslot].reshape(npg * FH, LANES)
    suf_in = _dot3_lhs(lf2, msuf_ref[...])
    cross = jnp.sum(_dot3_rhs(msel_ref[...], lf2), axis=-1, keepdims=True)
    bias_scr[...] = (suf_in + cross).reshape(npg, FH, LANES)

    q = q_ref[0].astype(F32)
    rr = lax.broadcasted_iota(jnp.int32, (nrow, FW), 0) // t_new
    cc = lax.broadcasted_iota(jnp.int32, (nrow, FW), 1) // FD
    qbd = jnp.where(rr == cc, jnp.concatenate([q] * FH, axis=0), 0.0).astype(BF16)
    rep = rep_ref[...]

    def bias_rows(bias8):
        n = bias8.shape[1]
        a, b2, c2 = _split3(bias8)
        return jnp.concatenate([a.astype(F32), b2.astype(F32), c2.astype(F32),
                                jnp.zeros((LANES - 3 * FH, n), F32)], axis=0).astype(BF16)

    pad = jnp.zeros((LANES - t_new, FW), F32)
    kn = jnp.concatenate([kn_ref[0], pad], axis=0).astype(BF16)
    vn = jnp.concatenate([vn_ref[0], pad], axis=0).astype(BF16)
    s = _dot_nt(qbd, kn) + _dot(rep, bias_rows(-cn_ref[0]))
    qpos = lax.broadcasted_iota(jnp.int32, (nrow, LANES), 0) % t_new
    kpos = lax.broadcasted_iota(jnp.int32, (nrow, LANES), 1)
    s = jnp.where(kpos <= qpos, s, NEG)
    m0 = jnp.max(s, axis=-1, keepdims=True)
    p = jnp.exp(s - m0)
    m_scr[...] = jnp.broadcast_to(m0, m_scr.shape)
    l_scr[...] = jnp.broadcast_to(jnp.sum(p, axis=-1, keepdims=True), l_scr.shape)
    acc_scr[...] = _dot(p.astype(BF16), vn)

    def chunk_body(c, carry):
        g = b * nch + c
        slot = g % nslot
        for cp in kv_copies(b, c, slot):
            cp.wait()

        ahead = g + (nslot - 1)

        @pl.when(ahead < nseq * nch)
        def _():
            nb, ncx = chunk_of(ahead)
            for cp in kv_copies(nb, ncx, ahead % nslot):
                cp.start()

        kt = chunk_tile(kbuf, slot)
        blk = bias_scr[pl.ds(c * ch, ch)]
        bias8 = jnp.concatenate([blk[pi] for pi in range(ch)], axis=1)
        s = _dot(qbd, kt) + _dot(rep, bias_rows(bias8))
        m_old = m_scr[:, 0:1]
        m_new = jnp.maximum(m_old, jnp.max(s, axis=-1, keepdims=True))
        p = jnp.exp(s - m_new)
        alpha = jnp.exp(m_old - m_new)
        vt = chunk_tile(vbuf, slot)
        acc_scr[...] = alpha * acc_scr[...] + _dot_nt(p.astype(BF16), vt)
        l_scr[...] = alpha * l_scr[...] + jnp.sum(p, axis=-1, keepdims=True)
        m_scr[...] = jnp.broadcast_to(m_new, m_scr.shape)
        return carry

    lax.fori_loop(0, nch, chunk_body, 0)

    accn = acc_scr[...] / l_scr[:, 0:1]
    out = jnp.zeros((t_new, FW), F32)
    hcol = lax.broadcasted_iota(jnp.int32, (t_new, FW), 1) // FD
    for h in range(FH):
        out = jnp.where(hcol == h, accn[h * t_new:(h + 1) * t_new, :], out)
    o_ref[0] = out


def _fox_sample(page_table, q_s, k_new, v_new, c_new, rep, msuf, msel, lf_t, k_t, v_t, *, ch, nslot):
    nseq, t_new, _ = q_s.shape
    npg = page_table.shape[1]
    seqblk = lambda n: pl.BlockSpec((1, t_new, n), lambda b, pt: (b, 0, 0))
    cst = lambda shape: pl.BlockSpec(shape, lambda b, pt: (0,) * len(shape), pipeline_mode=pl.Buffered(1))
    anyspec = pl.BlockSpec(memory_space=pl.ANY)
    nrow = FH * t_new
    grid_spec = pltpu.PrefetchScalarGridSpec(
        num_scalar_prefetch=1,
        grid=(nseq,),
        in_specs=[seqblk(FW), seqblk(FW), seqblk(FW), pl.BlockSpec((1, FH, LANES), lambda b, pt: (b, 0, 0)),
                  cst((nrow, LANES)), cst((LANES, LANES)), cst((npg * FH, npg * FH)), anyspec, anyspec, anyspec],
        out_specs=seqblk(FW),
        scratch_shapes=[pltpu.VMEM((nslot, ch, FH, FD, LANES), F32), pltpu.VMEM((nslot, ch, FH, FD, LANES), F32),
                        pltpu.VMEM((2, npg, FH, LANES), F32), pltpu.VMEM((npg, FH, LANES), F32),
                        pltpu.VMEM((nrow, LANES), F32), pltpu.VMEM((nrow, LANES), F32), pltpu.VMEM((nrow, FW), F32),
                        pltpu.SemaphoreType.DMA((nslot,)), pltpu.SemaphoreType.DMA((nslot,)),
                        pltpu.SemaphoreType.DMA((2,))],
    )
    return pl.pallas_call(
        functools.partial(_fox_sample_body, nseq=nseq, npg=npg, ch=ch, t_new=t_new),
        grid_spec=grid_spec,
        out_shape=jax.ShapeDtypeStruct((nseq, t_new, FW), F32),
        compiler_params=_params(("arbitrary",)),
        name="fox_sample",
    )(page_table, q_s, k_new, v_new, c_new, rep, msuf, msel, lf_t, k_t, v_t)


def _ffn_body(x_ref, om_ref, af_ref, of_ref, buf_ref, wo_ref, gf_ref, wu_ref, wc_ref, bc_ref, wd_ref,
              y_ref, tail_ref, ext, *, tiles_per_seq, t_new):
    tm = x_ref.shape[0]
    if t_new is None:
        @pl.when(pl.program_id(0) % tiles_per_seq == 0)
        def _():
            ext[0:SUB, :] = jnp.zeros((SUB, D_FF), F32)

    mixin = jnp.concatenate([om_ref[...], (of_ref[...] * af_ref[...]).astype(BF16)], axis=1)
    h = x_ref[...] + _dot(mixin, wo_ref[...])
    hn = (h * lax.rsqrt(jnp.mean(h * h, axis=-1, keepdims=True) + EPS) * gf_ref[...]).astype(BF16)
    up_a = _dot(hn, wu_ref[:, 0:D_FF])
    up_g = _dot(hn, wu_ref[:, D_FF:2 * D_FF])
    if t_new is None:
        ext[SUB:SUB + tm, :] = up_a
        conv = wc_ref[0:1, :] * ext[pl.ds(SUB - 2, tm), :]
        for j in range(1, FFN_CONV):
            conv = conv + wc_ref[j:j + 1, :] * ext[pl.ds(SUB - 2 + j, tm), :]
        ext[0:SUB, :] = up_a[tm - SUB:tm, :]
        tail_ref[...] = up_a[tm - SUB:tm, :]
    else:
        nseg = tm // t_new
        cur = up_a.reshape(nseg, t_new, D_FF)
        buf = buf_ref[...].reshape(nseg, t_new, D_FF)
        tpos = lax.broadcasted_iota(jnp.int32, cur.shape, 1)
        w = lambda j: wc_ref[j:j + 1, :].reshape(1, 1, D_FF)
        conv = w(FFN_CONV - 1) * cur
        for d in range(1, FFN_CONV):
            shifted = jnp.where(tpos >= d, pltpu.roll(cur, d, axis=1), pltpu.roll(buf, d, axis=1))
            conv = conv + w(FFN_CONV - 1 - d) * shifted
        conv = conv.reshape(tm, D_FF)
        tail_ref[...] = up_a
    act = (_silu(conv + bc_ref[...]) * up_g).astype(BF16)
    y_ref[...] = h + _dot(act, wd_ref[...])


def _ffn(x2d, outm, af, of, bufpad, w_out, g_ffn, w_up, w_ffconv, b_ffconv, w_down, *, tm, tiles_per_seq, t_new):
    r = x2d.shape[0]
    row = lambda n: pl.BlockSpec((tm, n), lambda i: (i, 0))
    if t_new is None:
        tail_spec = pl.BlockSpec((SUB, D_FF), lambda i: (i, 0))
        tail_shape = jax.ShapeDtypeStruct((r // tm * SUB, D_FF), F32)
        buf_spec = pl.BlockSpec((SUB, D_FF), lambda i: (0, 0))
    else:
        tail_spec = row(D_FF)
        tail_shape = jax.ShapeDtypeStruct((r, D_FF), F32)
        buf_spec = row(D_FF)
    return pl.pallas_call(
        functools.partial(_ffn_body, tiles_per_seq=tiles_per_seq, t_new=t_new),
        grid=(r // tm,),
        in_specs=[row(D_MODEL), row(MW), row(FW), row(FW), buf_spec,
                  _const_spec((D_MODEL, D_MODEL)), _const_spec((1, D_MODEL)), _const_spec((D_MODEL, 2 * D_FF)),
                  _const_spec((FFN_CONV, D_FF)), _const_spec((1, D_FF)), _const_spec((D_FF, D_MODEL))],
        out_specs=[row(D_MODEL), tail_spec],
        out_shape=[jax.ShapeDtypeStruct((r, D_MODEL), F32), tail_shape],
        scratch_shapes=[pltpu.VMEM((SUB + tm, D_FF), F32)],
        compiler_params=_params(("arbitrary",)),
        name="ffn",
    )(x2d, outm, af, of, bufpad, w_out, g_ffn, w_up, w_ffconv, b_ffconv, w_down)


def _lane_pad(v, n=LANES):
    return jnp.pad(v, ((0, 0), (0, n - v.shape[1])))


def _state_rows(state, t_new):
    nseq, k1, c = state.shape
    return jnp.pad(state, ((0, 0), (t_new - k1, 0), (0, 0))).reshape(nseq * t_new, c)


def kernel(x_prompt, x_sample, cache_k, cache_v, cache_logf, page_table, state_C, state_n, state_m, state_qkconv,
           state_ffnconv, g_mix, w_in, b_gate, w_qkconv, g_qf, g_kf, g_mh, w_out, g_ffn, w_up, w_ffconv, b_ffconv,
           w_down):
    assert w_in.shape[0] == 1, "single-layer kernel"
    bsz, seq, _ = x_prompt.shape
    dbsz, t_new, _ = x_sample.shape
    tm = 256
    tm_ffn = 512

    w = w_in[0]
    w_main = w[:, 0:C_G].astype(BF16)
    w_gate = _lane_pad(w[:, C_G:]).astype(BF16)
    bg_pad = _lane_pad(b_gate.astype(F32))
    gq2 = jnp.tile(g_qf.astype(F32), (1, LANES // FD))
    gk2 = jnp.tile(g_kf.astype(F32), (1, LANES // FD))
    gmix = g_mix.astype(F32)
    wconv = w_qkconv[0].astype(F32)
    gmh = g_mh.astype(F32)
    wo_b, wu_b, wd_b = w_out[0].astype(BF16), w_up[0].astype(BF16), w_down[0].astype(BF16)
    gffn, wfc, bfc = g_ffn.astype(F32), w_ffconv[0].astype(F32), b_ffconv.astype(F32)

    ridx = jnp.arange(tm)
    tri_p = (ridx[None, :] <= ridx[:, None]).astype(BF16)
    tri_s = ((ridx[None, :] <= ridx[:, None]) & (ridx[None, :] // t_new == ridx[:, None] // t_new)).astype(BF16)

    xp = x_prompt.reshape(bsz * seq, D_MODEL).astype(F32)
    (qk_p, vm_p, om_p, qa_p, ka_p, va_p, of_p, g_p, _, kt_p, vt_p) = _proj(
        xp, gmix, w_main, w_gate, bg_pad, gq2, gk2, tri_p, tm=tm, tiles_per_seq=seq // tm, kv_transposed=True)
    outm_p, c_p, n_p, m_p = _mlstm_prompt(qk_p, vm_p, om_p, g_p, wconv, gmh, bsz=bsz, seq=seq)
    af_p = _fox_prompt(qa_p, ka_p, va_p, bsz=bsz, seq=seq, tq=512)
    y_p, tail_p = _ffn(xp, outm_p, af_p, of_p, jnp.zeros((SUB, D_FF), F32), wo_b, gffn, wu_b, wfc, bfc, wd_b,
                       tm=tm_ffn, tiles_per_seq=seq // tm_ffn, t_new=None)

    xs = x_sample.reshape(dbsz * t_new, D_MODEL).astype(F32)
    (qk_s, vm_s, om_s, qa_s, _, _, of_s, g_s, fc_s, kf_s, vf_s) = _proj(
        xs, gmix, w_main, w_gate, bg_pad, gq2, gk2, tri_s, tm=tm, tiles_per_seq=1, kv_transposed=False)
    mrows = _lane_pad(jnp.repeat(state_m[0].astype(F32), t_new, axis=0))
    nrows = jnp.repeat(state_n[0].astype(F32).reshape(dbsz, MW), t_new, axis=0)
    n_t = jnp.transpose(state_n[0].astype(F32), (1, 0, 2))
    outm_s, c_s, n_s_t, mrow_s = _mlstm_sample(
        qk_s, _state_rows(state_qkconv[0].astype(F32), t_new), vm_s, om_s, g_s, mrows, nrows,
        state_C[0].astype(F32), n_t, wconv, gmh, t_new=t_new)

    k_new = kf_s.reshape(dbsz, t_new, FH, FD)
    v_new = vf_s.reshape(dbsz, t_new, FH, FD)
    q_s = qa_s.reshape(dbsz, t_new, FH, LANES)[..., :FD].reshape(dbsz, t_new, FW)
    c_new = _lane_pad(jnp.transpose(fc_s.reshape(dbsz, t_new, LANES)[:, :, G_FF:G_FF + FH], (0, 2, 1))
                      .reshape(dbsz * FH, t_new)).reshape(dbsz, FH, LANES)
    npg = page_table.shape[1]
    nrow = FH * t_new
    rr = jnp.arange(nrow)[:, None] // t_new
    cc = jnp.arange(LANES)[None, :]
    rep = ((cc < 3 * FH) & (cc % FH == rr)).astype(BF16)
    kk = jnp.arange(LANES)
    msuf = (kk[:, None] > kk[None, :]).astype(BF16)
    pr = jnp.arange(npg * FH)
    msel = ((pr[None, :] % FH == pr[:, None] % FH) & (pr[None, :] // FH > pr[:, None] // FH)).astype(BF16)
    k_t = jnp.transpose(cache_k[0], (0, 2, 3, 1)).astype(F32)
    v_t = jnp.transpose(cache_v[0], (0, 2, 3, 1)).astype(F32)
    lf_t = jnp.transpose(cache_logf[0], (0, 2, 1)).astype(F32)
    fs_args = (page_table.astype(jnp.int32), q_s, k_new.reshape(dbsz, t_new, FW),
               v_new.reshape(dbsz, t_new, FW), c_new, rep, msuf, msel, lf_t, k_t, v_t)
    af_s = _fox_sample(*fs_args, ch=16, nslot=3)
    y_s, tail_s = _ffn(xs, outm_s, af_s.reshape(dbsz * t_new, FW), of_s,
                       _state_rows(state_ffnconv[0].astype(F32), t_new), wo_b, gffn, wu_b, wfc, bfc, wd_b,
                       tm=tm, tiles_per_seq=1, t_new=t_new)

    dt = x_prompt.dtype
    y_prompt = y_p.reshape(bsz, seq, D_MODEL).astype(dt)
    y_sample = y_s.reshape(dbsz, t_new, D_MODEL).astype(x_sample.dtype)
    k_prompt = jnp.transpose(kt_p, (0, 3, 1, 2))[None]
    v_prompt = jnp.transpose(vt_p, (0, 3, 1, 2))[None]
    logf_prompt = g_p.reshape(bsz, seq, LANES)[:, :, G_FF:G_FF + FH][None]
    c_prompt = c_p[None]
    n_prompt = n_p[:, :MH, :][None]
    m_prompt = m_p[:, :MH, 0][None]
    qkconv_prompt = qk_p.reshape(bsz, seq, 2 * MW)[:, seq - (QK_CONV - 1):, :][None]
    ffnconv_prompt = tail_p.reshape(bsz, seq // tm_ffn, SUB, D_FF)[:, -1, SUB - (FFN_CONV - 1):, :][None]
    k_sample = k_new[None]
    v_sample = v_new[None]
    logf_sample = g_s.reshape(dbsz, t_new, LANES)[:, :, G_FF:G_FF + FH][None]
    c_sample = c_s[None]
    n_sample = jnp.transpose(n_s_t, (1, 0, 2))[None]
    m_sample = mrow_s.reshape(dbsz, t_new, LANES)[:, t_new - 1, :MH][None]
    qkconv_sample = qk_s.reshape(dbsz, t_new, 2 * MW)[:, t_new - (QK_CONV - 1):, :][None]
    ffnconv_sample = tail_s.reshape(dbsz, t_new, D_FF)[:, t_new - (FFN_CONV - 1):, :][None]
    return (y_prompt, y_sample, k_prompt, v_prompt, logf_prompt, c_prompt, n_prompt, m_prompt, qkconv_prompt,
            ffnconv_prompt, k_sample, v_sample, logf_sample, c_sample, n_sample, m_sample, qkconv_sample,
            ffnconv_sample)
```

```python
import functools

import jax
import jax.numpy as jnp
from jax import lax
from jax.experimental import pallas as pl
from jax.experimental.pallas import tpu as pltpu

F32, BF16 = jnp.float32, jnp.bfloat16

D_MODEL = 1024
MH, MD = 4, 128
MW = MH * MD
FH, FD = 8, 64
FW = FH * FD
QK_CONV, FFN_CONV = 4, 3
D_FF = 2816
N_GATES = 2 * MH + FH
EPS = 1e-6
NEG = -1e30
LANES = 128
SUB = 8
VMEM_LIMIT = 56 * 1024 * 1024

C_QK, C_VM, C_OM = 0, 2 * MW, 3 * MW
C_QF = 4 * MW
C_KF, C_VF, C_OF, C_G = C_QF + FW, C_QF + 2 * FW, C_QF + 3 * FW, C_QF + 4 * FW
G_FM, G_FF = MH, 2 * MH


def _split3(x):
    a = x.astype(BF16)
    r = x - a.astype(F32)
    b = r.astype(BF16)
    c = (r - b.astype(F32)).astype(BF16)
    return a, b, c


def _dot(a, b):
    return jnp.dot(a, b, preferred_element_type=F32)


def _dot_nt(a, b):
    return lax.dot_general(a, b, (((1,), (1,)), ((), ())), preferred_element_type=F32)


def _dot_tn(a, b):
    return lax.dot_general(a, b, (((0,), (0,)), ((), ())), preferred_element_type=F32)


def _dot3_rhs(m, x):
    a, b, c = _split3(x)
    return _dot(m, c) + _dot(m, b) + _dot(m, a)


def _dot3_lhs(x, m):
    a, b, c = _split3(x)
    return _dot(c, m) + _dot(b, m) + _dot(a, m)


def _log_sigmoid(x):
    return jnp.minimum(x, 0.0) - jnp.log1p(jnp.exp(-jnp.abs(x)))


def _silu(x):
    return x * jax.nn.sigmoid(x)


def _const_spec(shape):
    n = len(shape)
    return pl.BlockSpec(shape, lambda *_: (0,) * n, pipeline_mode=pl.Buffered(1))


def _params(sem):
    return pltpu.CompilerParams(dimension_semantics=sem, vmem_limit_bytes=VMEM_LIMIT)


def _proj_body(x_ref, gmix_ref, w_ref, wg_ref, bg_ref, gq_ref, gk_ref, tri_ref,
               qk_ref, vm_ref, om_ref, qa_ref, ka_ref, va_ref, of_ref, g_ref, fc_ref, kf_ref, vf_ref,
               carry_ref, *, tiles_per_seq, kv_transposed):
    tm = x_ref.shape[0]
    if tiles_per_seq > 1:
        @pl.when(pl.program_id(0) % tiles_per_seq == 0)
        def _():
            carry_ref[...] = jnp.zeros_like(carry_ref)

    x = x_ref[...]
    ms = jnp.mean(x * x, axis=-1, keepdims=True)
    xn = (x * lax.rsqrt(ms + EPS) * gmix_ref[...]).astype(BF16)

    def proj(lo, n):
        return _dot(xn, w_ref[:, lo:lo + n])

    g = _dot(xn, wg_ref[...]) + bg_ref[...]
    col = lax.broadcasted_iota(jnp.int32, g.shape, 1)
    gates = jnp.where(col < MH, g, jnp.where(col < N_GATES, _log_sigmoid(g), 0.0))
    g_ref[...] = gates

    cum = _dot3_rhs(tri_ref[...], gates)
    if tiles_per_seq > 1:
        cum = cum + carry_ref[...]
        carry_ref[...] = cum[tm - 1:tm, :]
    fc_ref[...] = cum

    nf = -cum
    n1 = nf.astype(BF16).astype(F32)
    r1 = nf - n1
    n2 = r1.astype(BF16).astype(F32)
    n3 = (r1 - n2).astype(BF16).astype(F32)

    pq = proj(C_QF, FW)
    pk = proj(C_KF, FW)
    pv = proj(C_VF, FW)
    qk_ref[...] = proj(C_QK, 2 * MW)
    vm_ref[...] = proj(C_VM, MW).astype(BF16)
    om_ref[...] = jax.nn.sigmoid(proj(C_OM, MW))
    of_ref[...] = jax.nn.sigmoid(proj(C_OF, FW))
    lane = lax.broadcasted_iota(jnp.int32, (tm, LANES), 1)
    low = lane < FD
    gq = gq_ref[...]
    gk = gk_ref[...]

    def pair_norm(x, g2):
        sq = x * x
        ms_lo = jnp.sum(jnp.where(low, sq, 0.0), axis=-1, keepdims=True) * (1.0 / FD)
        ms_hi = jnp.sum(jnp.where(low, 0.0, sq), axis=-1, keepdims=True) * (1.0 / FD)
        return x * jnp.where(low, lax.rsqrt(ms_lo + EPS), lax.rsqrt(ms_hi + EPS)) * g2

    def store_f32(ref, j, pair):
        if kv_transposed:
            tr = jnp.concatenate([pair[r0:r0 + LANES, :].T for r0 in range(0, tm, LANES)], axis=1)
            ref[0, 2 * j] = tr[0:FD, :]
            ref[0, 2 * j + 1] = tr[FD:2 * FD, :]
        else:
            ref[:, j * LANES:(j + 1) * LANES] = pair

    for j in range(FH // 2):
        psl = slice(j * LANES, (j + 1) * LANES)
        qn2 = pair_norm(pq[:, psl], gq) * (FD ** -0.5)
        kn2 = pair_norm(pk[:, psl], gk)
        xv2 = pv[:, psl]
        store_f32(kf_ref, j, kn2)
        store_f32(vf_ref, j, xv2)
        for odd in range(2):
            h = 2 * j + odd
            sl = slice(h * LANES, (h + 1) * LANES)
            qn, kn, xv = ((pltpu.roll(a, FD, axis=1) if odd else a) for a in (qn2, kn2, xv2))
            c = G_FF + h
            b1 = jnp.broadcast_to(n1[:, c:c + 1], (tm, LANES))
            b2 = jnp.broadcast_to(n2[:, c:c + 1], (tm, LANES))
            b3 = jnp.broadcast_to(n3[:, c:c + 1], (tm, LANES))
            zero = jnp.zeros((tm, LANES), F32)
            qa = jnp.where(low, qn, jnp.where(lane < FD + 3, 1.0, zero))
            ka = jnp.where(low, kn, jnp.where(lane == FD, b1, jnp.where(lane == FD + 1, b2,
                                                                       jnp.where(lane == FD + 2, b3, zero))))
            va = jnp.where(low, xv, jnp.where(lane == FD, 1.0, zero))
            qa_ref[:, sl] = qa.astype(BF16)
            ka_ref[:, sl] = ka.astype(BF16)
            va_ref[:, sl] = va.astype(BF16)


def _proj(x2d, gmix, w_main, w_gate, bg_pad, gq2, gk2, tri, *, tm, tiles_per_seq, kv_transposed):
    r = x2d.shape[0]
    row = lambda n: pl.BlockSpec((tm, n), lambda i: (i, 0))
    widths = [(2 * MW, F32), (MW, BF16), (MW, F32), (FH * LANES, BF16), (FH * LANES, BF16), (FH * LANES, BF16),
              (FW, F32), (LANES, F32), (LANES, F32)]
    out_specs = [row(n) for n, _ in widths]
    out_shape = [jax.ShapeDtypeStruct((r, n), dt) for n, dt in widths]
    if kv_transposed:
        nseq = r // (tm * tiles_per_seq)
        kv_spec = pl.BlockSpec((1, FH, FD, tm), lambda i: (i // tiles_per_seq, 0, 0, i % tiles_per_seq))
        kv_shape = jax.ShapeDtypeStruct((nseq, FH, FD, tm * tiles_per_seq), F32)
    else:
        kv_spec, kv_shape = row(FW), jax.ShapeDtypeStruct((r, FW), F32)
    return pl.pallas_call(
        functools.partial(_proj_body, tiles_per_seq=tiles_per_seq, kv_transposed=kv_transposed),
        grid=(r // tm,),
        in_specs=[row(D_MODEL), _const_spec((1, D_MODEL)), _const_spec((D_MODEL, C_G)), _const_spec((D_MODEL, LANES)),
                  _const_spec((1, LANES)), _const_spec((1, LANES)), _const_spec((1, LANES)), _const_spec((tm, tm))],
        out_specs=out_specs + [kv_spec, kv_spec],
        out_shape=out_shape + [kv_shape, kv_shape],
        scratch_shapes=[pltpu.VMEM((1, LANES), F32)],
        compiler_params=_params(("arbitrary",)),
        name="proj",
    )(x2d, gmix, w_main, w_gate, bg_pad, gq2, gk2, tri)


def _mlstm_gate_math(h, gates, g_t, cum, cum_t, tot, tot_t, mp_col, cmask, smask):
    i_col, i_row = gates[:, h:h + 1], g_t[h:h + 1, :]
    c = G_FM + h
    b_col, b_row = cum[:, c:c + 1], cum_t[c:c + 1, :]
    t_col, t_row = tot[:, c:c + 1], tot_t[c:c + 1, :]
    logd = jnp.where(cmask, b_col + (i_row - b_row), NEG)
    m_t = jnp.maximum(b_col + mp_col, jnp.max(logd, axis=-1, keepdims=True))
    z = jnp.where(smask, (t_row - b_row) + i_row, NEG)
    m_new = jnp.maximum(t_col + mp_col, jnp.max(z, axis=-1, keepdims=True))
    dmat = jnp.exp(logd - m_t)
    sc = jnp.exp(b_col + mp_col - m_t)
    w_col = jnp.exp(t_col - b_col + i_col - m_new)
    decay = jnp.exp(t_col + mp_col - m_new)
    return dmat, sc, m_t, m_new, w_col, decay


def _mlstm_gate_math_one_seq(h, gates, g_t, cum, cum_t, mp, cmask):
    L = gates.shape[0]
    i_col, i_row = gates[:, h:h + 1], g_t[h:h + 1, :]
    c = G_FM + h
    b_col, b_row = cum[:, c:c + 1], cum_t[c:c + 1, :]
    tot = b_col[L - 1:L, :]
    logd = jnp.where(cmask, b_col + (i_row - b_row), NEG)
    m_t = jnp.maximum(b_col + mp, jnp.max(logd, axis=-1, keepdims=True))
    m_new = jnp.maximum(tot + mp, jnp.max((tot - b_row) + i_row, axis=-1, keepdims=True))
    dmat = jnp.exp(logd - m_t)
    sc = jnp.exp(b_col + mp - m_t)
    w_col = jnp.exp(tot - b_col + i_col - m_new)
    decay = jnp.exp(tot + mp - m_new)
    return dmat, sc, m_t, m_new, w_col, decay


def _mlstm_head_out(q_bf, k_bf, v_bf, dmat, sc, m_t, inter, qn, om, gmh):
    s = _dot_nt(q_bf, k_bf) * dmat
    num = _dot(s.astype(BF16), v_bf) + sc * inter
    den = jnp.sum(s, axis=-1, keepdims=True) + sc * qn
    hc = num / jnp.maximum(jnp.abs(den), jnp.exp(-m_t))
    hn = hc * lax.rsqrt(jnp.mean(hc * hc, axis=-1, keepdims=True) + EPS) * gmh
    return (om * hn).astype(BF16)


def _mlstm_prompt_body(qk_ref, halo_ref, vm_ref, om_ref, g_ref, wc_ref, gmh_ref,
                       out_ref, c_out, n_out, m_out,
                       ext, c_scr, n_scr, m_scr):
    c_idx = pl.program_id(1)
    L = qk_ref.shape[0]

    @pl.when(c_idx == 0)
    def _():
        c_scr[...] = jnp.zeros_like(c_scr)
        n_scr[...] = jnp.zeros_like(n_scr)
        m_scr[...] = jnp.zeros_like(m_scr)

    ext[0:SUB, :] = jnp.where(c_idx == 0, 0.0, halo_ref[...])
    ext[SUB:SUB + L, :] = qk_ref[...]
    conv = wc_ref[0:1, :] * ext[pl.ds(SUB - 3, L), :]
    for j in range(1, QK_CONV):
        conv = conv + wc_ref[j:j + 1, :] * ext[pl.ds(SUB - 3 + j, L), :]
    qk = _silu(conv)

    row = lax.broadcasted_iota(jnp.int32, (L, L), 0)
    colm = lax.broadcasted_iota(jnp.int32, (L, L), 1)
    cmask = colm <= row
    tri = jnp.where(cmask, 1.0, 0.0).astype(BF16)

    gates = g_ref[...]
    cum = _dot3_rhs(tri, gates)
    g_t, cum_t = gates.T, cum.T

    for h in range(MH):
        sl = slice(h * MD, (h + 1) * MD)
        q = qk[:, sl]
        k = qk[:, MW + h * MD:MW + (h + 1) * MD] * (MD ** -0.5)
        q_bf, k_bf = q.astype(BF16), k.astype(BF16)
        v_bf = vm_ref[:, sl]
        mp = m_scr[h:h + 1, 0:1]
        dmat, sc, m_t, m_new, w_col, dec = _mlstm_gate_math_one_seq(h, gates, g_t, cum, cum_t, mp, cmask)
        c_prev = c_scr[h]
        n_prev = n_scr[h:h + 1, :]
        inter = _dot_nt(q_bf, c_prev.astype(BF16))
        qn = jnp.sum(q * n_prev, axis=-1, keepdims=True)
        out_ref[:, sl] = _mlstm_head_out(q_bf, k_bf, v_bf, dmat, sc, m_t, inter, qn, om_ref[:, sl], gmh_ref[:, sl])
        wv = (w_col * v_bf.astype(F32)).astype(BF16)
        c_scr[h] = dec * c_prev + _dot_tn(wv, k_bf)
        n_scr[h:h + 1, :] = dec * n_prev + jnp.sum(w_col * k, axis=0, keepdims=True)
        m_scr[h:h + 1, :] = jnp.broadcast_to(m_new, (1, LANES))

    @pl.when(c_idx == pl.num_programs(1) - 1)
    def _():
        c_out[0] = c_scr[...]
        n_out[0] = n_scr[...]
        m_out[0] = m_scr[...]


def _mlstm_prompt(qk, vm, om, gates, wconv, gmh, *, bsz, seq):
    L = MD
    nck = seq // L
    blk = lambda n: pl.BlockSpec((L, n), lambda b, c: (b * nck + c, 0))
    halo = pl.BlockSpec((SUB, 2 * MW), lambda b, c: (jnp.maximum((b * nck + c) * (L // SUB) - 1, 0), 0))
    st = lambda shape: pl.BlockSpec((1,) + shape, lambda b, c: (b,) + (0,) * len(shape))
    return pl.pallas_call(
        _mlstm_prompt_body,
        grid=(bsz, nck),
        in_specs=[blk(2 * MW), halo, blk(MW), blk(MW), blk(LANES), _const_spec((QK_CONV, 2 * MW)), _const_spec((1, MW))],
        out_specs=[blk(MW), st((MH, MD, MD)), st((SUB, MD)), st((SUB, LANES))],
        out_shape=[jax.ShapeDtypeStruct((bsz * seq, MW), BF16),
                   jax.ShapeDtypeStruct((bsz, MH, MD, MD), F32),
                   jax.ShapeDtypeStruct((bsz, SUB, MD), F32),
                   jax.ShapeDtypeStruct((bsz, SUB, LANES), F32)],
        scratch_shapes=[pltpu.VMEM((SUB + L, 2 * MW), F32), pltpu.VMEM((MH, MD, MD), F32),
                        pltpu.VMEM((SUB, MD), F32), pltpu.VMEM((SUB, LANES), F32)],
        compiler_params=_params(("arbitrary", "arbitrary")),
        name="mlstm_prompt",
    )(qk, qk, vm, om, gates, wconv, gmh)


def _mlstm_sample_body(qk_ref, buf_ref, vm_ref, om_ref, g_ref, mrow_ref, nrow_ref, c_in, n_in, wc_ref, gmh_ref,
                       out_ref, c_out, n_out, mrow_out,
                       q_scr, k_scr, wv_scr, inter_scr, dec_scr, *, t_new):
    L = qk_ref.shape[0]
    nseg = L // t_new
    cur = qk_ref[...].reshape(nseg, t_new, 2 * MW)
    buf = buf_ref[...].reshape(nseg, t_new, 2 * MW)
    tpos = lax.broadcasted_iota(jnp.int32, cur.shape, 1)
    w = lambda j: wc_ref[j:j + 1, :].reshape(1, 1, 2 * MW)
    conv = w(QK_CONV - 1) * cur
    for d in range(1, QK_CONV):
        shifted = jnp.where(tpos >= d, pltpu.roll(cur, d, axis=1), pltpu.roll(buf, d, axis=1))
        conv = conv + w(QK_CONV - 1 - d) * shifted
    qk = _silu(conv).reshape(L, 2 * MW)

    row = lax.broadcasted_iota(jnp.int32, (L, L), 0)
    colm = lax.broadcasted_iota(jnp.int32, (L, L), 1)
    smask = (row // t_new) == (colm // t_new)
    cmask = smask & (colm <= row)
    tri = jnp.where(cmask, 1.0, 0.0).astype(BF16)
    ones = jnp.where(smask, 1.0, 0.0).astype(BF16)
    seg16 = jnp.where(lax.broadcasted_iota(jnp.int32, (nseg, L), 0)
                      == lax.broadcasted_iota(jnp.int32, (nseg, L), 1) // t_new, 1.0, 0.0).astype(BF16)

    gates = g_ref[...]
    cum = _dot3_rhs(tri, gates)
    tot = _dot3_rhs(ones, gates)
    g_t, cum_t, tot_t = gates.T, cum.T, tot.T
    lane = lax.broadcasted_iota(jnp.int32, (L, LANES), 1)

    head = []
    dec_all = jnp.zeros((L, LANES), F32)
    mnew_all = jnp.zeros((L, LANES), F32)
    for h in range(MH):
        sl = slice(h * MD, (h + 1) * MD)
        q = qk[:, sl]
        k = qk[:, MW + h * MD:MW + (h + 1) * MD] * (MD ** -0.5)
        v_bf = vm_ref[:, sl]
        mp = mrow_ref[:, h:h + 1]
        dmat, sc, m_t, m_new, w_col, decay = _mlstm_gate_math(h, gates, g_t, cum, cum_t, tot, tot_t, mp, cmask, smask)
        q_scr[h] = q
        k_scr[h] = k.astype(BF16)
        wv_scr[h] = w_col * v_bf.astype(F32)
        dec_all = jnp.where(lane == h, jnp.broadcast_to(decay, (L, LANES)), dec_all)
        mnew_all = jnp.where(lane == h, jnp.broadcast_to(m_new, (L, LANES)), mnew_all)
        n_add = _dot3_rhs(seg16, w_col * k)
        dec_seg = _dot3_rhs(seg16, decay * (1.0 / t_new) * jnp.ones((L, MD), F32))
        n_out[h] = dec_seg * n_in[h] + n_add
        qn = jnp.sum(q * nrow_ref[:, sl], axis=-1, keepdims=True)
        head.append((dmat, sc, m_t, qn))
    dec_scr[...] = dec_all
    mrow_out[...] = mnew_all

    rowseg = lax.broadcasted_iota(jnp.int32, (L, MD), 0) // t_new

    def seq_body(j, carry):
        r0 = pl.multiple_of(j * t_new, t_new)
        dgrp = dec_scr[pl.ds(r0, t_new), :]
        for h in range(MH):
            c_prev = c_in[j, h]
            q_j = q_scr[h, pl.ds(r0, t_new), :].astype(BF16)
            inter_scr[h, pl.ds(r0, t_new), :] = _dot_nt(q_j, c_prev.astype(BF16))
            wv_j = jnp.where(rowseg == j, wv_scr[h], 0.0).astype(BF16)
            c_out[j, h] = dgrp[0:1, h:h + 1] * c_prev + _dot_tn(wv_j, k_scr[h])
        return carry

    lax.fori_loop(0, nseg, seq_body, 0, unroll=4)

    for h in range(MH):
        sl = slice(h * MD, (h + 1) * MD)
        dmat, sc, m_t, qn = head[h]
        out_ref[:, sl] = _mlstm_head_out(q_scr[h].astype(BF16), k_scr[h], vm_ref[:, sl], dmat, sc, m_t, inter_scr[h], qn,
                                         om_ref[:, sl], gmh_ref[:, sl])


def _mlstm_sample(qk, bufpad, vm, om, gates, mrows, nrows, c_state, n_state_t, wconv, gmh, *, t_new):
    L = MD
    r = qk.shape[0]
    nseg = L // t_new
    blk = lambda n: pl.BlockSpec((L, n), lambda i: (i, 0))
    return pl.pallas_call(
        functools.partial(_mlstm_sample_body, t_new=t_new),
        grid=(r // L,),
        in_specs=[blk(2 * MW), blk(2 * MW), blk(MW), blk(MW), blk(LANES), blk(LANES), blk(MW),
                  pl.BlockSpec((nseg, MH, MD, MD), lambda i: (i, 0, 0, 0)),
                  pl.BlockSpec((MH, nseg, MD), lambda i: (0, i, 0)),
                  _const_spec((QK_CONV, 2 * MW)), _const_spec((1, MW))],
        out_specs=[blk(MW), pl.BlockSpec((nseg, MH, MD, MD), lambda i: (i, 0, 0, 0)),
                   pl.BlockSpec((MH, nseg, MD), lambda i: (0, i, 0)), blk(LANES)],
        out_shape=[jax.ShapeDtypeStruct((r, MW), BF16), jax.ShapeDtypeStruct(c_state.shape, F32),
                   jax.ShapeDtypeStruct(n_state_t.shape, F32), jax.ShapeDtypeStruct((r, LANES), F32)],
        scratch_shapes=[pltpu.VMEM((MH, L, MD), F32), pltpu.VMEM((MH, L, MD), BF16), pltpu.VMEM((MH, L, MD), F32),
                        pltpu.VMEM((MH, L, MD), F32), pltpu.VMEM((L, LANES), F32)],
        compiler_params=_params(("arbitrary",)),
        name="mlstm_sample",
    )(qk, bufpad, vm, om, gates, mrows, nrows, c_state, n_state_t, wconv, gmh)


def _fox_prompt_body(qa_ref, ka_ref, va_ref, o_ref, m_scr, acc_scr, s0_scr, s1_scr, *, tq):
    i = pl.program_id(2)
    row = lax.broadcasted_iota(jnp.int32, (tq, tq), 0)
    colm = lax.broadcasted_iota(jnp.int32, (tq, tq), 1)
    nheads = m_scr.shape[0]
    m_scr[...] = jnp.full(m_scr.shape, NEG, F32)
    acc_scr[...] = jnp.zeros_like(acc_scr)

    def rows(j):
        return pl.ds(pl.multiple_of(j * tq, tq), tq)

    def produce(j, s_scr):
        for hh in range(nheads):
            sl = slice(hh * LANES, (hh + 1) * LANES)
            s_scr[hh] = _dot_nt(qa_ref[:, sl], ka_ref[rows(j), sl])

    def consume(j, s_scr, masked):
        for hh in range(nheads):
            sl = slice(hh * LANES, (hh + 1) * LANES)
            s = s_scr[hh]
            if masked:
                s = jnp.where(colm <= row, s, NEG)
            m_old = m_scr[hh]
            m_new = jnp.maximum(m_old, jnp.max(s, axis=-1, keepdims=True))
            p = jnp.exp(s - jnp.concatenate([m_new] * (tq // LANES), axis=1))
            acc_scr[hh] = jnp.exp(m_old - m_new) * acc_scr[hh] + _dot(p.astype(BF16), va_ref[rows(j), sl])
            m_scr[hh] = m_new

    produce(0, s0_scr)

    def pair_body(t, carry):
        produce(2 * t + 1, s1_scr)
        consume(2 * t, s0_scr, False)
        produce(2 * t + 2, s0_scr)
        consume(2 * t + 1, s1_scr, False)
        return carry

    lax.fori_loop(0, i // 2, pair_body, 0)

    @pl.when(i % 2 == 0)
    def _():
        consume(i, s0_scr, True)

    @pl.when(i % 2 == 1)
    def _():
        produce(i, s1_scr)
        consume(i - 1, s0_scr, False)
        consume(i, s1_scr, True)

    outs = []
    for hh in range(nheads):
        acc = acc_scr[hh]
        outs.append(acc[:, 0:FD] / acc[:, FD:FD + 1])
    o_ref[...] = jnp.concatenate(outs, axis=1)


def _fox_prompt(qa, ka, va, *, bsz, seq, tq):
    nq = seq // tq
    w2 = 2 * LANES
    return pl.pallas_call(
        functools.partial(_fox_prompt_body, tq=tq),
        grid=(bsz, FH // 2, nq),
        in_specs=[pl.BlockSpec((tq, w2), lambda b, hp, i: (b * nq + i, hp)),
                  pl.BlockSpec((seq, w2), lambda b, hp, i: (b, hp)),
                  pl.BlockSpec((seq, w2), lambda b, hp, i: (b, hp))],
        out_specs=pl.BlockSpec((tq, 2 * FD), lambda b, hp, i: (b * nq + i, hp)),
        out_shape=jax.ShapeDtypeStruct((bsz * seq, FW), F32),
        scratch_shapes=[pltpu.VMEM((2, tq, LANES), F32), pltpu.VMEM((2, tq, LANES), F32),
                        pltpu.VMEM((2, tq, tq), F32), pltpu.VMEM((2, tq, tq), F32)],
        compiler_params=_params(("arbitrary", "arbitrary", "arbitrary")),
        name="fox_prompt",
    )(qa, ka, va)


def _fox_sample_body(pt_ref, q_ref, kn_ref, vn_ref, cn_ref, rep_ref, msuf_ref, msel_ref, lf_hbm, k_hbm, v_hbm,
                     o_ref,
                     kbuf, vbuf, lfbuf, bias_scr, m_scr, l_scr, acc_scr, ksem, vsem, lfsem,
                     *, nseq, npg, ch, t_new):
    b = pl.program_id(0)
    nch = npg // ch
    nslot = kbuf.shape[0]
    nrow = FH * t_new

    def chunk_of(g):
        return g // nch, g % nch

    def kv_copies(seq, c, slot):
        cps = []
        for pi in range(ch):
            pg = pt_ref[seq, c * ch + pi]
            cps.append(pltpu.make_async_copy(k_hbm.at[pg], kbuf.at[slot, pi], ksem.at[slot]))
            cps.append(pltpu.make_async_copy(v_hbm.at[pg], vbuf.at[slot, pi], vsem.at[slot]))
        return cps

    def chunk_tile(buf, slot):
        return jnp.concatenate([buf[slot, pi].reshape(FW, LANES) for pi in range(ch)], axis=1).astype(BF16)

    def lf_copy(seq, p, slot):
        return pltpu.make_async_copy(lf_hbm.at[pt_ref[seq, p]], lfbuf.at[slot, p], lfsem.at[slot])

    def lf_start(seq, slot):
        def body(p, carry):
            lf_copy(seq, p, slot).start()
            return carry
        lax.fori_loop(0, npg, body, 0)

    def lf_wait(seq, slot):
        def body(p, carry):
            lf_copy(seq, p, slot).wait()
            return carry
        lax.fori_loop(0, npg, body, 0)

    @pl.when(b == 0)
    def _():
        lf_start(0, 0)
        for g0 in range(nslot - 1):
            for cp in kv_copies(g0 // nch, g0 % nch, g0):
                cp.start()

    lslot = b % 2
    lf_wait(b, lslot)

    @pl.when(b + 1 < nseq)
    def _():
        lf_start(b + 1, 1 - lslot)

    lf2 = lfbuf[lslot].reshape(npg * FH, LANES)
    suf_in = _dot3_lhs(lf2, msuf_ref[...])
    cross = jnp.sum(_dot3_rhs(msel_ref[...], lf2), axis=-1, keepdims=True)
    bias_scr[...] = (suf_in + cross).reshape(npg, FH, LANES)

    q = q_ref[0].astype(F32)
    rr = lax.broadcasted_iota(jnp.int32, (nrow, FW), 0) // t_new
    cc = lax.broadcasted_iota(jnp.int32, (nrow, FW), 1) // FD
    qbd = jnp.where(rr == cc, jnp.concatenate([q] * FH, axis=0), 0.0).astype(BF16)
    rep = rep_ref[...]

    def bias_rows(bias8):
        n = bias8.shape[1]
        a, b2, c2 = _split3(bias8)
        return jnp.concatenate([a.astype(F32), b2.astype(F32), c2.astype(F32),
                                jnp.zeros((LANES - 3 * FH, n), F32)], axis=0).astype(BF16)

    pad = jnp.zeros((LANES - t_new, FW), F32)
    kn = jnp.concatenate([kn_ref[0], pad], axis=0).astype(BF16)
    vn = jnp.concatenate([vn_ref[0], pad], axis=0).astype(BF16)
    s = _dot_nt(qbd, kn) + _dot(rep, bias_rows(-cn_ref[0]))
    qpos = lax.broadcasted_iota(jnp.int32, (nrow, LANES), 0) % t_new
    kpos = lax.broadcasted_iota(jnp.int32, (nrow, LANES), 1)
    s = jnp.where(kpos <= qpos, s, NEG)
    m0 = jnp.max(s, axis=-1, keepdims=True)
    p = jnp.exp(s - m0)
    m_scr[...] = jnp.broadcast_to(m0, m_scr.shape)
    l_scr[...] = jnp.broadcast_to(jnp.sum(p, axis=-1, keepdims=True), l_scr.shape)
    acc_scr[...] = _dot(p.astype(BF16), vn)

    def chunk_body(c, carry):
        g = b * nch + c
        slot = g % nslot
        for cp in kv_copies(b, c, slot):
            cp.wait()

        ahead = g + (nslot - 1)

        @pl.when(ahead < nseq * nch)
        def _():
            nb, ncx = chunk_of(ahead)
            for cp in kv_copies(nb, ncx, ahead % nslot):
                cp.start()

        kt = chunk_tile(kbuf, slot)
        blk = bias_scr[pl.ds(c * ch, ch)]
        bias8 = jnp.concatenate([blk[pi] for pi in range(ch)], axis=1)
        s = _dot(qbd, kt) + _dot(rep, bias_rows(bias8))
        m_old = m_scr[:, 0:1]
        m_new = jnp.maximum(m_old, jnp.max(s, axis=-1, keepdims=True))
        p = jnp.exp(s - m_new)
        alpha = jnp.exp(m_old - m_new)
        vt = chunk_tile(vbuf, slot)
        acc_scr[...] = alpha * acc_scr[...] + _dot_nt(p.astype(BF16), vt)
        l_scr[...] = alpha * l_scr[...] + jnp.sum(p, axis=-1, keepdims=True)
        m_scr[...] = jnp.broadcast_to(m_new, m_scr.shape)
        return carry

    lax.fori_loop(0, nch, chunk_body, 0)

    accn = acc_scr[...] / l_scr[:, 0:1]
    out = jnp.zeros((t_new, FW), F32)
    hcol = lax.broadcasted_iota(jnp.int32, (t_new, FW), 1) // FD
    for h in range(FH):
        out = jnp.where(hcol == h, accn[h * t_new:(h + 1) * t_new, :], out)
    o_ref[0] = out


def _fox_sample(page_table, q_s, k_new, v_new, c_new, rep, msuf, msel, lf_t, k_t, v_t, *, ch, nslot):
    nseq, t_new, _ = q_s.shape
    npg = page_table.shape[1]
    seqblk = lambda n: pl.BlockSpec((1, t_new, n), lambda b, pt: (b, 0, 0))
    cst = lambda shape: pl.BlockSpec(shape, lambda b, pt: (0,) * len(shape), pipeline_mode=pl.Buffered(1))
    anyspec = pl.BlockSpec(memory_space=pl.ANY)
    nrow = FH * t_new
    grid_spec = pltpu.PrefetchScalarGridSpec(
        num_scalar_prefetch=1,
        grid=(nseq,),
        in_specs=[seqblk(FW), seqblk(FW), seqblk(FW), pl.BlockSpec((1, FH, LANES), lambda b, pt: (b, 0, 0)),
                  cst((nrow, LANES)), cst((LANES, LANES)), cst((npg * FH, npg * FH)), anyspec, anyspec, anyspec],
        out_specs=seqblk(FW),
        scratch_shapes=[pltpu.VMEM((nslot, ch, FH, FD, LANES), F32), pltpu.VMEM((nslot, ch, FH, FD, LANES), F32),
                        pltpu.VMEM((2, npg, FH, LANES), F32), pltpu.VMEM((npg, FH, LANES), F32),
                        pltpu.VMEM((nrow, LANES), F32), pltpu.VMEM((nrow, LANES), F32), pltpu.VMEM((nrow, FW), F32),
                        pltpu.SemaphoreType.DMA((nslot,)), pltpu.SemaphoreType.DMA((nslot,)),
                        pltpu.SemaphoreType.DMA((2,))],
    )
    return pl.pallas_call(
        functools.partial(_fox_sample_body, nseq=nseq, npg=npg, ch=ch, t_new=t_new),
        grid_spec=grid_spec,
        out_shape=jax.ShapeDtypeStruct((nseq, t_new, FW), F32),
        compiler_params=_params(("arbitrary",)),
        name="fox_sample",
    )(page_table, q_s, k_new, v_new, c_new, rep, msuf, msel, lf_t, k_t, v_t)


def _ffn_body(x_ref, om_ref, af_ref, of_ref, buf_ref, wo_ref, gf_ref, wu_ref, wc_ref, bc_ref, wd_ref,
              y_ref, tail_ref, ext, *, tiles_per_seq, t_new):
    tm = x_ref.shape[0]
    if t_new is None:
        @pl.when(pl.program_id(0) % tiles_per_seq == 0)
        def _():
            ext[0:SUB, :] = jnp.zeros((SUB, D_FF), F32)

    mixin = jnp.concatenate([om_ref[...], (of_ref[...] * af_ref[...]).astype(BF16)], axis=1)
    h = x_ref[...] + _dot(mixin, wo_ref[...])
    hn = (h * lax.rsqrt(jnp.mean(h * h, axis=-1, keepdims=True) + EPS) * gf_ref[...]).astype(BF16)
    up_a = _dot(hn, wu_ref[:, 0:D_FF])
    up_g = _dot(hn, wu_ref[:, D_FF:2 * D_FF])
    if t_new is None:
        ext[SUB:SUB + tm, :] = up_a
        conv = wc_ref[0:1, :] * ext[pl.ds(SUB - 2, tm), :]
        for j in range(1, FFN_CONV):
            conv = conv + wc_ref[j:j + 1, :] * ext[pl.ds(SUB - 2 + j, tm), :]
        ext[0:SUB, :] = up_a[tm - SUB:tm, :]
        tail_ref[...] = up_a[tm - SUB:tm, :]
    else:
        nseg = tm // t_new
        cur = up_a.reshape(nseg, t_new, D_FF)
        buf = buf_ref[...].reshape(nseg, t_new, D_FF)
        tpos = lax.broadcasted_iota(jnp.int32, cur.shape, 1)
        w = lambda j: wc_ref[j:j + 1, :].reshape(1, 1, D_FF)
        conv = w(FFN_CONV - 1) * cur
        for d in range(1, FFN_CONV):
            shifted = jnp.where(tpos >= d, pltpu.roll(cur, d, axis=1), pltpu.roll(buf, d, axis=1))
            conv = conv + w(FFN_CONV - 1 - d) * shifted
        conv = conv.reshape(tm, D_FF)
        tail_ref[...] = up_a
    act = (_silu(conv + bc_ref[...]) * up_g).astype(BF16)
    y_ref[...] = h + _dot(act, wd_ref[...])


def _ffn(x2d, outm, af, of, bufpad, w_out, g_ffn, w_up, w_ffconv, b_ffconv, w_down, *, tm, tiles_per_seq, t_new):
    r = x2d.shape[0]
    row = lambda n: pl.BlockSpec((tm, n), lambda i: (i, 0))
    if t_new is None:
        tail_spec = pl.BlockSpec((SUB, D_FF), lambda i: (i, 0))
        tail_shape = jax.ShapeDtypeStruct((r // tm * SUB, D_FF), F32)
        buf_spec = pl.BlockSpec((SUB, D_FF), lambda i: (0, 0))
    else:
        tail_spec = row(D_FF)
        tail_shape = jax.ShapeDtypeStruct((r, D_FF), F32)
        buf_spec = row(D_FF)
    return pl.pallas_call(
        functools.partial(_ffn_body, tiles_per_seq=tiles_per_seq, t_new=t_new),
        grid=(r // tm,),
        in_specs=[row(D_MODEL), row(MW), row(FW), row(FW), buf_spec,
                  _const_spec((D_MODEL, D_MODEL)), _const_spec((1, D_MODEL)), _const_spec((D_MODEL, 2 * D_FF)),
                  _const_spec((FFN_CONV, D_FF)), _const_spec((1, D_FF)), _const_spec((D_FF, D_MODEL))],
        out_specs=[row(D_MODEL), tail_spec],
        out_shape=[jax.ShapeDtypeStruct((r, D_MODEL), F32), tail_shape],
        scratch_shapes=[pltpu.VMEM((SUB + tm, D_FF), F32)],
        compiler_params=_params(("arbitrary",)),
        name="ffn",
    )(x2d, outm, af, of, bufpad, w_out, g_ffn, w_up, w_ffconv, b_ffconv, w_down)


def _lane_pad(v, n=LANES):
    return jnp.pad(v, ((0, 0), (0, n - v.shape[1])))


def _state_rows(state, t_new):
    nseq, k1, c = state.shape
    return jnp.pad(state, ((0, 0), (t_new - k1, 0), (0, 0))).reshape(nseq * t_new, c)


def kernel(x_prompt, x_sample, cache_k, cache_v, cache_logf, page_table, state_C, state_n, state_m, state_qkconv,
           state_ffnconv, g_mix, w_in, b_gate, w_qkconv, g_qf, g_kf, g_mh, w_out, g_ffn, w_up, w_ffconv, b_ffconv,
           w_down):
    assert w_in.shape[0] == 1, "single-layer kernel"
    bsz, seq, _ = x_prompt.shape
    dbsz, t_new, _ = x_sample.shape
    tm = 256
    tm_ffn = 512

    w = w_in[0]
    w_main = w.astype(BF16)
    w_gate = _lane_pad(w[:, C_G:]).astype(BF16)
    bg_pad = _lane_pad(b_gate.astype(F32))
    gq2 = jnp.tile(g_qf.astype(F32), (1, LANES // FD))
    gk2 = jnp.tile(g_kf.astype(F32), (1, LANES // FD))
    gmix = g_mix.astype(F32)
    wconv = w_qkconv[0].astype(F32)
    gmh = g_mh.astype(F32)
    wo_b, wu_b, wd_b = w_out[0].astype(BF16), w_up[0].astype(BF16), w_down[0].astype(BF16)
    gffn, wfc, bfc = g_ffn.astype(F32), w_ffconv[0].astype(F32), b_ffconv.astype(F32)

    ridx = jnp.arange(tm)
    tri_p = (ridx[None, :] <= ridx[:, None]).astype(BF16)
    tri_s = ((ridx[None, :] <= ridx[:, None]) & (ridx[None, :] // t_new == ridx[:, None] // t_new)).astype(BF16)

    xp = x_prompt.reshape(bsz * seq, D_MODEL).astype(F32)
    (qk_p, vm_p, om_p, qa_p, ka_p, va_p, of_p, g_p, _, kt_p, vt_p) = _proj(
        xp, gmix, w_main, w_gate, bg_pad, gq2, gk2, tri_p, tm=tm, tiles_per_seq=seq // tm, kv_transposed=True)
    outm_p, c_p, n_p, m_p = _mlstm_prompt(qk_p, vm_p, om_p, g_p, wconv, gmh, bsz=bsz, seq=seq)
    af_p = _fox_prompt(qa_p, ka_p, va_p, bsz=bsz, seq=seq, tq=512)
    y_p, tail_p = _ffn(xp, outm_p, af_p, of_p, jnp.zeros((SUB, D_FF), F32), wo_b, gffn, wu_b, wfc, bfc, wd_b,
                       tm=tm_ffn, tiles_per_seq=seq // tm_ffn, t_new=None)

    xs = x_sample.reshape(dbsz * t_new, D_MODEL).astype(F32)
    (qk_s, vm_s, om_s, qa_s, _, _, of_s, g_s, fc_s, kf_s, vf_s) = _proj(
        xs, gmix, w_main, w_gate, bg_pad, gq2, gk2, tri_s, tm=tm, tiles_per_seq=1, kv_transposed=False)
    mrows = _lane_pad(jnp.repeat(state_m[0].astype(F32), t_new, axis=0))
    nrows = jnp.repeat(state_n[0].astype(F32).reshape(dbsz, MW), t_new, axis=0)
    n_t = jnp.transpose(state_n[0].astype(F32), (1, 0, 2))
    outm_s, c_s, n_s_t, mrow_s = _mlstm_sample(
        qk_s, _state_rows(state_qkconv[0].astype(F32), t_new), vm_s, om_s, g_s, mrows, nrows,
        state_C[0].astype(F32), n_t, wconv, gmh, t_new=t_new)

    k_new = kf_s.reshape(dbsz, t_new, FH, FD)
    v_new = vf_s.reshape(dbsz, t_new, FH, FD)
    q_s = qa_s.reshape(dbsz, t_new, FH, LANES)[..., :FD].reshape(dbsz, t_new, FW)
    c_new = _lane_pad(jnp.transpose(fc_s.reshape(dbsz, t_new, LANES)[:, :, G_FF:G_FF + FH], (0, 2, 1))
                      .reshape(dbsz * FH, t_new)).reshape(dbsz, FH, LANES)
    npg = page_table.shape[1]
    nrow = FH * t_new
    rr = jnp.arange(nrow)[:, None] // t_new
    cc = jnp.arange(LANES)[None, :]
    rep = ((cc < 3 * FH) & (cc % FH == rr)).astype(BF16)
    kk = jnp.arange(LANES)
    msuf = (kk[:, None] > kk[None, :]).astype(BF16)
    pr = jnp.arange(npg * FH)
    msel = ((pr[None, :] % FH == pr[:, None] % FH) & (pr[None, :] // FH > pr[:, None] // FH)).astype(BF16)
    k_t = jnp.transpose(cache_k[0], (0, 2, 3, 1)).astype(F32)
    v_t = jnp.transpose(cache_v[0], (0, 2, 3, 1)).astype(F32)
    lf_t = jnp.transpose(cache_logf[0], (0, 2, 1)).astype(F32)
    fs_args = (page_table.astype(jnp.int32), q_s, k_new.reshape(dbsz, t_new, FW),
               v_new.reshape(dbsz, t_new, FW), c_new, rep, msuf, msel, lf_t, k_t, v_t)
    af_s = _fox_sample(*fs_args, ch=16, nslot=3)
    y_s, tail_s = _ffn(xs, outm_s, af_s.reshape(dbsz * t_new, FW), of_s,
                       _state_rows(state_ffnconv[0].astype(F32), t_new), wo_b, gffn, wu_b, wfc, bfc, wd_b,
                       tm=tm, tiles_per_seq=1, t_new=t_new)

    dt = x_prompt.dtype
    y_prompt = y_p.reshape(bsz, seq, D_MODEL).astype(dt)
    y_sample = y_s.reshape(dbsz, t_new, D_MODEL).astype(x_sample.dtype)
    k_prompt = jnp.transpose(kt_p, (0, 3, 1, 2))[None]
    v_prompt = jnp.transpose(vt_p, (0, 3, 1, 2))[None]
    logf_prompt = g_p.reshape(bsz, seq, LANES)[:, :, G_FF:G_FF + FH][None]
    c_prompt = c_p[None]
    n_prompt = n_p[:, :MH, :][None]
    m_prompt = m_p[:, :MH, 0][None]
    qkconv_prompt = qk_p.reshape(bsz, seq, 2 * MW)[:, seq - (QK_CONV - 1):, :][None]
    ffnconv_prompt = tail_p.reshape(bsz, seq // tm_ffn, SUB, D_FF)[:, -1, SUB - (FFN_CONV - 1):, :][None]
    k_sample = k_new[None]
    v_sample = v_new[None]
    logf_sample = g_s.reshape(dbsz, t_new, LANES)[:, :, G_FF:G_FF + FH][None]
    c_sample = c_s[None]
    n_sample = jnp.transpose(n_s_t, (1, 0, 2))[None]
    m_sample = mrow_s.reshape(dbsz, t_new, LANES)[:, t_new - 1, :MH][None]
    qkconv_sample = qk_s.reshape(dbsz, t_new, 2 * MW)[:, t_new - (QK_CONV - 1):, :][None]
    ffnconv_sample = tail_s.reshape(dbsz, t_new, D_FF)[:, t_new - (FFN_CONV - 1):, :][None]
    return (y_prompt, y_sample, k_prompt, v_prompt, logf_prompt, c_prompt, n_prompt, m_prompt, qkconv_prompt,
            ffnconv_prompt, k_sample, v_sample, logf_sample, c_sample, n_sample, m_sample, qkconv_sample,
            ffnconv_sample)
```

```python
import functools

import jax
import jax.numpy as jnp
from jax import lax
from jax.experimental import pallas as pl
from jax.experimental.pallas import tpu as pltpu

F32, BF16 = jnp.float32, jnp.bfloat16

D_MODEL = 1024
MH, MD = 4, 128
MW = MH * MD
FH, FD = 8, 64
FW = FH * FD
QK_CONV, FFN_CONV = 4, 3
D_FF = 2816
N_GATES = 2 * MH + FH
EPS = 1e-6
NEG = -1e30
LANES = 128
SUB = 8
VMEM_LIMIT = 56 * 1024 * 1024

C_QK, C_VM, C_OM = 0, 2 * MW, 3 * MW
C_QF = 4 * MW
C_KF, C_VF, C_OF, C_G = C_QF + FW, C_QF + 2 * FW, C_QF + 3 * FW, C_QF + 4 * FW
G_FM, G_FF = MH, 2 * MH


def _split3(x):
    a = x.astype(BF16)
    r = x - a.astype(F32)
    b = r.astype(BF16)
    c = (r - b.astype(F32)).astype(BF16)
    return a, b, c


def _dot(a, b):
    return jnp.dot(a, b, preferred_element_type=F32)


def _dot_nt(a, b):
    return lax.dot_general(a, b, (((1,), (1,)), ((), ())), preferred_element_type=F32)


def _dot_tn(a, b):
    return lax.dot_general(a, b, (((0,), (0,)), ((), ())), preferred_element_type=F32)


def _dot3_rhs(m, x):
    a, b, c = _split3(x)
    return _dot(m, c) + _dot(m, b) + _dot(m, a)


def _dot3_lhs(x, m):
    a, b, c = _split3(x)
    return _dot(c, m) + _dot(b, m) + _dot(a, m)


def _log_sigmoid(x):
    return jnp.minimum(x, 0.0) - jnp.log1p(jnp.exp(-jnp.abs(x)))


def _silu(x):
    return x * jax.nn.sigmoid(x)


def _const_spec(shape):
    n = len(shape)
    return pl.BlockSpec(shape, lambda *_: (0,) * n, pipeline_mode=pl.Buffered(1))


def _params(sem):
    return pltpu.CompilerParams(dimension_semantics=sem, vmem_limit_bytes=VMEM_LIMIT)


def _proj_body(x_ref, gmix_ref, w_ref, wg_ref, bg_ref, gq_ref, gk_ref, tri_ref,
               qk_ref, vm_ref, om_ref, qa_ref, ka_ref, va_ref, of_ref, g_ref, fc_ref, kf_ref, vf_ref,
               carry_ref, *, tiles_per_seq, kv_transposed):
    tm = x_ref.shape[0]
    if tiles_per_seq > 1:
        @pl.when(pl.program_id(0) % tiles_per_seq == 0)
        def _():
            carry_ref[...] = jnp.zeros_like(carry_ref)

    x = x_ref[...]
    ms = jnp.mean(x * x, axis=-1, keepdims=True)
    xn = (x * lax.rsqrt(ms + EPS) * gmix_ref[...]).astype(BF16)

    def proj(lo, n):
        return _dot(xn, w_ref[:, lo:lo + n])

    g = _dot(xn, wg_ref[...]) + bg_ref[...]
    col = lax.broadcasted_iota(jnp.int32, g.shape, 1)
    gates = jnp.where(col < MH, g, jnp.where(col < N_GATES, _log_sigmoid(g), 0.0))
    g_ref[...] = gates

    cum = _dot3_rhs(tri_ref[...], gates)
    if tiles_per_seq > 1:
        cum = cum + carry_ref[...]
        carry_ref[...] = cum[tm - 1:tm, :]
    fc_ref[...] = cum

    nf = -cum
    n1 = nf.astype(BF16).astype(F32)
    r1 = nf - n1
    n2 = r1.astype(BF16).astype(F32)
    n3 = (r1 - n2).astype(BF16).astype(F32)

    pq = proj(C_QF, FW)
    pk = proj(C_KF, FW)
    pv = proj(C_VF, FW)
    qk_ref[...] = proj(C_QK, 2 * MW)
    vm_ref[...] = proj(C_VM, MW).astype(BF16)
    om_ref[...] = jax.nn.sigmoid(proj(C_OM, MW))
    of_ref[...] = jax.nn.sigmoid(proj(C_OF, FW))
    lane = lax.broadcasted_iota(jnp.int32, (tm, LANES), 1)
    low = lane < FD
    gq = gq_ref[...]
    gk = gk_ref[...]

    def pair_norm(x, g2):
        sq = x * x
        ms_lo = jnp.sum(jnp.where(low, sq, 0.0), axis=-1, keepdims=True) * (1.0 / FD)
        ms_hi = jnp.sum(jnp.where(low, 0.0, sq), axis=-1, keepdims=True) * (1.0 / FD)
        return x * jnp.where(low, lax.rsqrt(ms_lo + EPS), lax.rsqrt(ms_hi + EPS)) * g2

    def store_f32(ref, j, pair):
        if kv_transposed:
            tr = jnp.concatenate([pair[r0:r0 + LANES, :].T for r0 in range(0, tm, LANES)], axis=1)
            ref[0, 2 * j] = tr[0:FD, :]
            ref[0, 2 * j + 1] = tr[FD:2 * FD, :]
        else:
            ref[:, j * LANES:(j + 1) * LANES] = pair

    for j in range(FH // 2):
        psl = slice(j * LANES, (j + 1) * LANES)
        qn2 = pair_norm(pq[:, psl], gq) * (FD ** -0.5)
        kn2 = pair_norm(pk[:, psl], gk)
        xv2 = pv[:, psl]
        store_f32(kf_ref, j, kn2)
        store_f32(vf_ref, j, xv2)
        for odd in range(2):
            h = 2 * j + odd
            sl = slice(h * LANES, (h + 1) * LANES)
            qn, kn, xv = ((pltpu.roll(a, FD, axis=1) if odd else a) for a in (qn2, kn2, xv2))
            c = G_FF + h
            b1 = jnp.broadcast_to(n1[:, c:c + 1], (tm, LANES))
            b2 = jnp.broadcast_to(n2[:, c:c + 1], (tm, LANES))
            b3 = jnp.broadcast_to(n3[:, c:c + 1], (tm, LANES))
            zero = jnp.zeros((tm, LANES), F32)
            qa = jnp.where(low, qn, jnp.where(lane < FD + 3, 1.0, zero))
            ka = jnp.where(low, kn, jnp.where(lane == FD, b1, jnp.where(lane == FD + 1, b2,
                                                                       jnp.where(lane == FD + 2, b3, zero))))
            va = jnp.where(low, xv, jnp.where(lane == FD, 1.0, zero))
            qa_ref[:, sl] = qa.astype(BF16)
            ka_ref[:, sl] = ka.astype(BF16)
            va_ref[:, sl] = va.astype(BF16)


def _proj(x2d, gmix, w_main, w_gate, bg_pad, gq2, gk2, tri, *, tm, tiles_per_seq, kv_transposed):
    r = x2d.shape[0]
    row = lambda n: pl.BlockSpec((tm, n), lambda i: (i, 0))
    widths = [(2 * MW, F32), (MW, BF16), (MW, F32), (FH * LANES, BF16), (FH * LANES, BF16), (FH * LANES, BF16),
              (FW, F32), (LANES, F32), (LANES, F32)]
    out_specs = [row(n) for n, _ in widths]
    out_shape = [jax.ShapeDtypeStruct((r, n), dt) for n, dt in widths]
    if kv_transposed:
        nseq = r // (tm * tiles_per_seq)
        kv_spec = pl.BlockSpec((1, FH, FD, tm), lambda i: (i // tiles_per_seq, 0, 0, i % tiles_per_seq))
        kv_shape = jax.ShapeDtypeStruct((nseq, FH, FD, tm * tiles_per_seq), F32)
    else:
        kv_spec, kv_shape = row(FW), jax.ShapeDtypeStruct((r, FW), F32)
    return pl.pallas_call(
        functools.partial(_proj_body, tiles_per_seq=tiles_per_seq, kv_transposed=kv_transposed),
        grid=(r // tm,),
        in_specs=[row(D_MODEL), _const_spec((1, D_MODEL)), _const_spec((D_MODEL, C_G)), _const_spec((D_MODEL, LANES)),
                  _const_spec((1, LANES)), _const_spec((1, LANES)), _const_spec((1, LANES)), _const_spec((tm, tm))],
        out_specs=out_specs + [kv_spec, kv_spec],
        out_shape=out_shape + [kv_shape, kv_shape],
        scratch_shapes=[pltpu.VMEM((1, LANES), F32)],
        compiler_params=_params(("arbitrary",)),
        name="proj",
    )(x2d, gmix, w_main, w_gate, bg_pad, gq2, gk2, tri)


def _mlstm_gate_math(h, gates, g_t, cum, cum_t, tot, tot_t, mp_col, cmask, smask):
    i_col, i_row = gates[:, h:h + 1], g_t[h:h + 1, :]
    c = G_FM + h
    b_col, b_row = cum[:, c:c + 1], cum_t[c:c + 1, :]
    t_col, t_row = tot[:, c:c + 1], tot_t[c:c + 1, :]
    logd = jnp.where(cmask, b_col + (i_row - b_row), NEG)
    m_t = jnp.maximum(b_col + mp_col, jnp.max(logd, axis=-1, keepdims=True))
    z = jnp.where(smask, (t_row - b_row) + i_row, NEG)
    m_new = jnp.maximum(t_col + mp_col, jnp.max(z, axis=-1, keepdims=True))
    dmat = jnp.exp(logd - m_t)
    sc = jnp.exp(b_col + mp_col - m_t)
    w_col = jnp.exp(t_col - b_col + i_col - m_new)
    decay = jnp.exp(t_col + mp_col - m_new)
    return dmat, sc, m_t, m_new, w_col, decay


def _mlstm_gate_math_one_seq(h, gates, g_t, cum, cum_t, mp, cmask):
    L = gates.shape[0]
    i_col, i_row = gates[:, h:h + 1], g_t[h:h + 1, :]
    c = G_FM + h
    b_col, b_row = cum[:, c:c + 1], cum_t[c:c + 1, :]
    tot = b_col[L - 1:L, :]
    logd = jnp.where(cmask, b_col + (i_row - b_row), NEG)
    m_t = jnp.maximum(b_col + mp, jnp.max(logd, axis=-1, keepdims=True))
    m_new = jnp.maximum(tot + mp, jnp.max((tot - b_row) + i_row, axis=-1, keepdims=True))
    dmat = jnp.exp(logd - m_t)
    sc = jnp.exp(b_col + mp - m_t)
    w_col = jnp.exp(tot - b_col + i_col - m_new)
    decay = jnp.exp(tot + mp - m_new)
    return dmat, sc, m_t, m_new, w_col, decay


def _mlstm_head_out(q_bf, k_bf, v_bf, dmat, sc, m_t, inter, qn, om, gmh):
    s = _dot_nt(q_bf, k_bf) * dmat
    num = _dot(s.astype(BF16), v_bf) + sc * inter
    den = jnp.sum(s, axis=-1, keepdims=True) + sc * qn
    hc = num / jnp.maximum(jnp.abs(den), jnp.exp(-m_t))
    hn = hc * lax.rsqrt(jnp.mean(hc * hc, axis=-1, keepdims=True) + EPS) * gmh
    return (om * hn).astype(BF16)


def _mlstm_prompt_body(qk_ref, halo_ref, vm_ref, om_ref, g_ref, wc_ref, gmh_ref,
                       out_ref, c_out, n_out, m_out,
                       ext, c_scr, n_scr, m_scr):
    c_idx = pl.program_id(1)
    L = qk_ref.shape[0]

    @pl.when(c_idx == 0)
    def _():
        c_scr[...] = jnp.zeros_like(c_scr)
        n_scr[...] = jnp.zeros_like(n_scr)
        m_scr[...] = jnp.zeros_like(m_scr)

    ext[0:SUB, :] = jnp.where(c_idx == 0, 0.0, halo_ref[...])
    ext[SUB:SUB + L, :] = qk_ref[...]
    conv = wc_ref[0:1, :] * ext[pl.ds(SUB - 3, L), :]
    for j in range(1, QK_CONV):
        conv = conv + wc_ref[j:j + 1, :] * ext[pl.ds(SUB - 3 + j, L), :]
    qk = _silu(conv)

    row = lax.broadcasted_iota(jnp.int32, (L, L), 0)
    colm = lax.broadcasted_iota(jnp.int32, (L, L), 1)
    cmask = colm <= row
    tri = jnp.where(cmask, 1.0, 0.0).astype(BF16)

    gates = g_ref[...]
    cum = _dot3_rhs(tri, gates)
    g_t, cum_t = gates.T, cum.T

    for h in range(MH):
        sl = slice(h * MD, (h + 1) * MD)
        q = qk[:, sl]
        k = qk[:, MW + h * MD:MW + (h + 1) * MD] * (MD ** -0.5)
        q_bf, k_bf = q.astype(BF16), k.astype(BF16)
        v_bf = vm_ref[:, sl]
        mp = m_scr[h:h + 1, 0:1]
        dmat, sc, m_t, m_new, w_col, dec = _mlstm_gate_math_one_seq(h, gates, g_t, cum, cum_t, mp, cmask)
        c_prev = c_scr[h]
        n_prev = n_scr[h:h + 1, :]
        inter = _dot_nt(q_bf, c_prev.astype(BF16))
        qn = jnp.sum(q * n_prev, axis=-1, keepdims=True)
        out_ref[:, sl] = _mlstm_head_out(q_bf, k_bf, v_bf, dmat, sc, m_t, inter, qn, om_ref[:, sl], gmh_ref[:, sl])
        wv = (w_col * v_bf.astype(F32)).astype(BF16)
        c_scr[h] = dec * c_prev + _dot_tn(wv, k_bf)
        n_scr[h:h + 1, :] = dec * n_prev + jnp.sum(w_col * k, axis=0, keepdims=True)
        m_scr[h:h + 1, :] = jnp.broadcast_to(m_new, (1, LANES))

    @pl.when(c_idx == pl.num_programs(1) - 1)
    def _():
        c_out[0] = c_scr[...]
        n_out[0] = n_scr[...]
        m_out[0] = m_scr[...]


def _mlstm_prompt(qk, vm, om, gates, wconv, gmh, *, bsz, seq):
    L = MD
    nck = seq // L
    blk = lambda n: pl.BlockSpec((L, n), lambda b, c: (b * nck + c, 0))
    halo = pl.BlockSpec((SUB, 2 * MW), lambda b, c: (jnp.maximum((b * nck + c) * (L // SUB) - 1, 0), 0))
    st = lambda shape: pl.BlockSpec((1,) + shape, lambda b, c: (b,) + (0,) * len(shape))
    return pl.pallas_call(
        _mlstm_prompt_body,
        grid=(bsz, nck),
        in_specs=[blk(2 * MW), halo, blk(MW), blk(MW), blk(LANES), _const_spec((QK_CONV, 2 * MW)), _const_spec((1, MW))],
        out_specs=[blk(MW), st((MH, MD, MD)), st((SUB, MD)), st((SUB, LANES))],
        out_shape=[jax.ShapeDtypeStruct((bsz * seq, MW), BF16),
                   jax.ShapeDtypeStruct((bsz, MH, MD, MD), F32),
                   jax.ShapeDtypeStruct((bsz, SUB, MD), F32),
                   jax.ShapeDtypeStruct((bsz, SUB, LANES), F32)],
        scratch_shapes=[pltpu.VMEM((SUB + L, 2 * MW), F32), pltpu.VMEM((MH, MD, MD), F32),
                        pltpu.VMEM((SUB, MD), F32), pltpu.VMEM((SUB, LANES), F32)],
        compiler_params=_params(("arbitrary", "arbitrary")),
        name="mlstm_prompt",
    )(qk, qk, vm, om, gates, wconv, gmh)


def _mlstm_sample_body(qk_ref, buf_ref, vm_ref, om_ref, g_ref, mrow_ref, nrow_ref, c_in, n_in, wc_ref, gmh_ref,
                       out_ref, c_out, n_out, mrow_out,
                       q_scr, k_scr, wv_scr, inter_scr, dec_scr, *, t_new):
    L = qk_ref.shape[0]
    nseg = L // t_new
    cur = qk_ref[...].reshape(nseg, t_new, 2 * MW)
    buf = buf_ref[...].reshape(nseg, t_new, 2 * MW)
    tpos = lax.broadcasted_iota(jnp.int32, cur.shape, 1)
    w = lambda j: wc_ref[j:j + 1, :].reshape(1, 1, 2 * MW)
    conv = w(QK_CONV - 1) * cur
    for d in range(1, QK_CONV):
        shifted = jnp.where(tpos >= d, pltpu.roll(cur, d, axis=1), pltpu.roll(buf, d, axis=1))
        conv = conv + w(QK_CONV - 1 - d) * shifted
    qk = _silu(conv).reshape(L, 2 * MW)

    row = lax.broadcasted_iota(jnp.int32, (L, L), 0)
    colm = lax.broadcasted_iota(jnp.int32, (L, L), 1)
    smask = (row // t_new) == (colm // t_new)
    cmask = smask & (colm <= row)
    tri = jnp.where(cmask, 1.0, 0.0).astype(BF16)
    ones = jnp.where(smask, 1.0, 0.0).astype(BF16)
    seg16 = jnp.where(lax.broadcasted_iota(jnp.int32, (nseg, L), 0)
                      == lax.broadcasted_iota(jnp.int32, (nseg, L), 1) // t_new, 1.0, 0.0).astype(BF16)

    gates = g_ref[...]
    cum = _dot3_rhs(tri, gates)
    tot = _dot3_rhs(ones, gates)
    g_t, cum_t, tot_t = gates.T, cum.T, tot.T
    lane = lax.broadcasted_iota(jnp.int32, (L, LANES), 1)

    head = []
    dec_all = jnp.zeros((L, LANES), F32)
    mnew_all = jnp.zeros((L, LANES), F32)
    for h in range(MH):
        sl = slice(h * MD, (h + 1) * MD)
        q = qk[:, sl]
        k = qk[:, MW + h * MD:MW + (h + 1) * MD] * (MD ** -0.5)
        v_bf = vm_ref[:, sl]
        mp = mrow_ref[:, h:h + 1]
        dmat, sc, m_t, m_new, w_col, decay = _mlstm_gate_math(h, gates, g_t, cum, cum_t, tot, tot_t, mp, cmask, smask)
        q_scr[h] = q
        k_scr[h] = k.astype(BF16)
        wv_scr[h] = w_col * v_bf.astype(F32)
        dec_all = jnp.where(lane == h, jnp.broadcast_to(decay, (L, LANES)), dec_all)
        mnew_all = jnp.where(lane == h, jnp.broadcast_to(m_new, (L, LANES)), mnew_all)
        n_add = _dot3_rhs(seg16, w_col * k)
        dec_seg = _dot3_rhs(seg16, decay * (1.0 / t_new) * jnp.ones((L, MD), F32))
        n_out[h] = dec_seg * n_in[h] + n_add
        qn = jnp.sum(q * nrow_ref[:, sl], axis=-1, keepdims=True)
        head.append((dmat, sc, m_t, qn))
    dec_scr[...] = dec_all
    mrow_out[...] = mnew_all

    rowseg = lax.broadcasted_iota(jnp.int32, (L, MD), 0) // t_new

    def seq_body(j, carry):
        r0 = pl.multiple_of(j * t_new, t_new)
        dgrp = dec_scr[pl.ds(r0, t_new), :]
        for h in range(MH):
            c_prev = c_in[j, h]
            q_j = q_scr[h, pl.ds(r0, t_new), :].astype(BF16)
            inter_scr[h, pl.ds(r0, t_new), :] = _dot_nt(q_j, c_prev.astype(BF16))
            wv_j = jnp.where(rowseg == j, wv_scr[h], 0.0).astype(BF16)
            c_out[j, h] = dgrp[0:1, h:h + 1] * c_prev + _dot_tn(wv_j, k_scr[h])
        return carry

    lax.fori_loop(0, nseg, seq_body, 0, unroll=4)

    for h in range(MH):
        sl = slice(h * MD, (h + 1) * MD)
        dmat, sc, m_t, qn = head[h]
        out_ref[:, sl] = _mlstm_head_out(q_scr[h].astype(BF16), k_scr[h], vm_ref[:, sl], dmat, sc, m_t, inter_scr[h], qn,
                                         om_ref[:, sl], gmh_ref[:, sl])


def _mlstm_sample(qk, bufpad, vm, om, gates, mrows, nrows, c_state, n_state_t, wconv, gmh, *, t_new):
    L = MD
    r = qk.shape[0]
    nseg = L // t_new
    blk = lambda n: pl.BlockSpec((L, n), lambda i: (i, 0))
    return pl.pallas_call(
        functools.partial(_mlstm_sample_body, t_new=t_new),
        grid=(r // L,),
        in_specs=[blk(2 * MW), blk(2 * MW), blk(MW), blk(MW), blk(LANES), blk(LANES), blk(MW),
                  pl.BlockSpec((nseg, MH, MD, MD), lambda i: (i, 0, 0, 0)),
                  pl.BlockSpec((MH, nseg, MD), lambda i: (0, i, 0)),
                  _const_spec((QK_CONV, 2 * MW)), _const_spec((1, MW))],
        out_specs=[blk(MW), pl.BlockSpec((nseg, MH, MD, MD), lambda i: (i, 0, 0, 0)),
                   pl.BlockSpec((MH, nseg, MD), lambda i: (0, i, 0)), blk(LANES)],
        out_shape=[jax.ShapeDtypeStruct((r, MW), BF16), jax.ShapeDtypeStruct(c_state.shape, F32),
                   jax.ShapeDtypeStruct(n_state_t.shape, F32), jax.ShapeDtypeStruct((r, LANES), F32)],
        scratch_shapes=[pltpu.VMEM((MH, L, MD), F32), pltpu.VMEM((MH, L, MD), BF16), pltpu.VMEM((MH, L, MD), F32),
                        pltpu.VMEM((MH, L, MD), F32), pltpu.VMEM((L, LANES), F32)],
        compiler_params=_params(("arbitrary",)),
        name="mlstm_sample",
    )(qk, bufpad, vm, om, gates, mrows, nrows, c_state, n_state_t, wconv, gmh)


def _fox_prompt_body(qa_ref, ka_ref, va_ref, o_ref, m_scr, acc_scr, s0_scr, s1_scr, *, tq):
    i = pl.program_id(2)
    row = lax.broadcasted_iota(jnp.int32, (tq, tq), 0)
    colm = lax.broadcasted_iota(jnp.int32, (tq, tq), 1)
    nheads = m_scr.shape[0]
    m_scr[...] = jnp.full(m_scr.shape, NEG, F32)
    acc_scr[...] = jnp.zeros_like(acc_scr)

    def rows(j):
        return pl.ds(pl.multiple_of(j * tq, tq), tq)

    def produce(j, s_scr):
        for hh in range(nheads):
            sl = slice(hh * LANES, (hh + 1) * LANES)
            s_scr[hh] = _dot_nt(qa_ref[:, sl], ka_ref[rows(j), sl])

    def consume(j, s_scr, masked):
        for hh in range(nheads):
            sl = slice(hh * LANES, (hh + 1) * LANES)
            s = s_scr[hh]
            if masked:
                s = jnp.where(colm <= row, s, NEG)
            m_old = m_scr[hh]
            m_new = jnp.maximum(m_old, jnp.max(s, axis=-1, keepdims=True))
            p = jnp.exp(s - jnp.concatenate([m_new] * (tq // LANES), axis=1))
            acc_scr[hh] = jnp.exp(m_old - m_new) * acc_scr[hh] + _dot(p.astype(BF16), va_ref[rows(j), sl])
            m_scr[hh] = m_new

    produce(0, s0_scr)

    def pair_body(t, carry):
        produce(2 * t + 1, s1_scr)
        consume(2 * t, s0_scr, False)
        produce(2 * t + 2, s0_scr)
        consume(2 * t + 1, s1_scr, False)
        return carry

    def quad_body(u, carry):
        pair_body(2 * u, carry)
        pair_body(2 * u + 1, carry)
        return carry

    lax.fori_loop(0, i // 4, quad_body, 0)

    @pl.when((i // 2) % 2 == 1)
    def _():
        pair_body(i // 2 - 1, 0)

    @pl.when(i % 2 == 0)
    def _():
        consume(i, s0_scr, True)

    @pl.when(i % 2 == 1)
    def _():
        produce(i, s1_scr)
        consume(i - 1, s0_scr, False)
        consume(i, s1_scr, True)

    outs = []
    for hh in range(nheads):
        acc = acc_scr[hh]
        outs.append(acc[:, 0:FD] / acc[:, FD:FD + 1])
    o_ref[...] = jnp.concatenate(outs, axis=1)


def _fox_prompt(qa, ka, va, *, bsz, seq, tq):
    nq = seq // tq
    nh = 4
    w2 = nh * LANES
    kv_spec = pl.BlockSpec((seq, w2), lambda b, hp, i: (b, hp), pipeline_mode=pl.Buffered(1))
    return pl.pallas_call(
        functools.partial(_fox_prompt_body, tq=tq),
        grid=(bsz, FH // nh, nq),
        in_specs=[pl.BlockSpec((tq, w2), lambda b, hp, i: (b * nq + i, hp)), kv_spec, kv_spec],
        out_specs=pl.BlockSpec((tq, nh * FD), lambda b, hp, i: (b * nq + i, hp)),
        out_shape=jax.ShapeDtypeStruct((bsz * seq, FW), F32),
        scratch_shapes=[pltpu.VMEM((nh, tq, LANES), F32), pltpu.VMEM((nh, tq, LANES), F32),
                        pltpu.VMEM((nh, tq, tq), F32), pltpu.VMEM((nh, tq, tq), F32)],
        compiler_params=_params(("arbitrary", "arbitrary", "arbitrary")),
        name="fox_prompt",
    )(qa, ka, va)


def _fox_sample_body(pt_ref, q_ref, kn_ref, vn_ref, cn_ref, rep_ref, msuf_ref, msel_ref, lf_hbm, k_hbm, v_hbm,
                     o_ref,
                     kbuf, vbuf, lfbuf, bias_scr, m_scr, l_scr, acc_scr, ksem, vsem, lfsem,
                     *, nseq, npg, ch, t_new):
    b = pl.program_id(0)
    nch = npg // ch
    nslot = kbuf.shape[0]
    nrow = FH * t_new

    def chunk_of(g):
        return g // nch, g % nch

    def kv_copies(seq, c, slot):
        cps = []
        for pi in range(ch):
            pg = pt_ref[seq, c * ch + pi]
            cps.append(pltpu.make_async_copy(k_hbm.at[pg], kbuf.at[slot, pi], ksem.at[slot]))
            cps.append(pltpu.make_async_copy(v_hbm.at[pg], vbuf.at[slot, pi], vsem.at[slot]))
        return cps

    def chunk_tile(buf, slot):
        return jnp.concatenate([buf[slot, pi].reshape(FW, LANES) for pi in range(ch)], axis=1).astype(BF16)

    def lf_copy(seq, p, slot):
        return pltpu.make_async_copy(lf_hbm.at[pt_ref[seq, p]], lfbuf.at[slot, p], lfsem.at[slot])

    def lf_start(seq, slot):
        def body(p, carry):
            lf_copy(seq, p, slot).start()
            return carry
        lax.fori_loop(0, npg, body, 0)

    def lf_wait(seq, slot):
        def body(p, carry):
            lf_copy(seq, p, slot).wait()
            return carry
        lax.fori_loop(0, npg, body, 0)

    @pl.when(b == 0)
    def _():
        lf_start(0, 0)
        for g0 in range(nslot - 1):
            for cp in kv_copies(g0 // nch, g0 % nch, g0):
                cp.start()

    lslot = b % 2
    lf_wait(b, lslot)

    @pl.when(b + 1 < nseq)
    def _():
        lf_start(b + 1, 1 - lslot)

    lf2 = lfbuf[lslot].reshape(npg * FH, LANES)
    suf_in = _dot3_lhs(lf2, msuf_ref[...])
    cross = jnp.sum(_dot3_rhs(msel_ref[...], lf2), axis=-1, keepdims=True)
    bias_scr[...] = (suf_in + cross).reshape(npg, FH, LANES)

    q = q_ref[0].astype(F32)
    rr = lax.broadcasted_iota(jnp.int32, (nrow, FW), 0) // t_new
    cc = lax.broadcasted_iota(jnp.int32, (nrow, FW), 1) // FD
    qbd = jnp.where(rr == cc, jnp.concatenate([q] * FH, axis=0), 0.0).astype(BF16)
    rep = rep_ref[...]

    def bias_rows(bias8):
        n = bias8.shape[1]
        a, b2, c2 = _split3(bias8)
        return jnp.concatenate([a.astype(F32), b2.astype(F32), c2.astype(F32),
                                jnp.zeros((LANES - 3 * FH, n), F32)], axis=0).astype(BF16)

    pad = jnp.zeros((LANES - t_new, FW), F32)
    kn = jnp.concatenate([kn_ref[0], pad], axis=0).astype(BF16)
    vn = jnp.concatenate([vn_ref[0], pad], axis=0).astype(BF16)
    s = _dot_nt(qbd, kn) + _dot(rep, bias_rows(-cn_ref[0]))
    qpos = lax.broadcasted_iota(jnp.int32, (nrow, LANES), 0) % t_new
    kpos = lax.broadcasted_iota(jnp.int32, (nrow, LANES), 1)
    s = jnp.where(kpos <= qpos, s, NEG)
    m0 = jnp.max(s, axis=-1, keepdims=True)
    p = jnp.exp(s - m0)
    m_scr[...] = jnp.broadcast_to(m0, m_scr.shape)
    l_scr[...] = jnp.broadcast_to(jnp.sum(p, axis=-1, keepdims=True), l_scr.shape)
    acc_scr[...] = _dot(p.astype(BF16), vn)

    def chunk_body(c, carry):
        g = b * nch + c
        slot = g % nslot
        for cp in kv_copies(b, c, slot):
            cp.wait()

        ahead = g + (nslot - 1)

        @pl.when(ahead < nseq * nch)
        def _():
            nb, ncx = chunk_of(ahead)
            for cp in kv_copies(nb, ncx, ahead % nslot):
                cp.start()

        kt = chunk_tile(kbuf, slot)
        blk = bias_scr[pl.ds(c * ch, ch)]
        bias8 = jnp.concatenate([blk[pi] for pi in range(ch)], axis=1)
        s = _dot(qbd, kt) + _dot(rep, bias_rows(bias8))
        m_old = m_scr[:, 0:1]
        m_new = jnp.maximum(m_old, jnp.max(s, axis=-1, keepdims=True))
        p = jnp.exp(s - m_new)
        alpha = jnp.exp(m_old - m_new)
        vt = chunk_tile(vbuf, slot)
        acc_scr[...] = alpha * acc_scr[...] + _dot_nt(p.astype(BF16), vt)
        l_scr[...] = alpha * l_scr[...] + jnp.sum(p, axis=-1, keepdims=True)
        m_scr[...] = jnp.broadcast_to(m_new, m_scr.shape)
        return carry

    lax.fori_loop(0, nch, chunk_body, 0)

    accn = acc_scr[...] / l_scr[:, 0:1]
    out = jnp.zeros((t_new, FW), F32)
    hcol = lax.broadcasted_iota(jnp.int32, (t_new, FW), 1) // FD
    for h in range(FH):
        out = jnp.where(hcol == h, accn[h * t_new:(h + 1) * t_new, :], out)
    o_ref[0] = out


def _fox_sample(page_table, q_s, k_new, v_new, c_new, rep, msuf, msel, lf_t, k_t, v_t, *, ch, nslot):
    nseq, t_new, _ = q_s.shape
    npg = page_table.shape[1]
    seqblk = lambda n: pl.BlockSpec((1, t_new, n), lambda b, pt: (b, 0, 0))
    cst = lambda shape: pl.BlockSpec(shape, lambda b, pt: (0,) * len(shape), pipeline_mode=pl.Buffered(1))
    anyspec = pl.BlockSpec(memory_space=pl.ANY)
    nrow = FH * t_new
    grid_spec = pltpu.PrefetchScalarGridSpec(
        num_scalar_prefetch=1,
        grid=(nseq,),
        in_specs=[seqblk(FW), seqblk(FW), seqblk(FW), pl.BlockSpec((1, FH, LANES), lambda b, pt: (b, 0, 0)),
                  cst((nrow, LANES)), cst((LANES, LANES)), cst((npg * FH, npg * FH)), anyspec, anyspec, anyspec],
        out_specs=seqblk(FW),
        scratch_shapes=[pltpu.VMEM((nslot, ch, FH, FD, LANES), F32), pltpu.VMEM((nslot, ch, FH, FD, LANES), F32),
                        pltpu.VMEM((2, npg, FH, LANES), F32), pltpu.VMEM((npg, FH, LANES), F32),
                        pltpu.VMEM((nrow, LANES), F32), pltpu.VMEM((nrow, LANES), F32), pltpu.VMEM((nrow, FW), F32),
                        pltpu.SemaphoreType.DMA((nslot,)), pltpu.SemaphoreType.DMA((nslot,)),
                        pltpu.SemaphoreType.DMA((2,))],
    )
    return pl.pallas_call(
        functools.partial(_fox_sample_body, nseq=nseq, npg=npg, ch=ch, t_new=t_new),
        grid_spec=grid_spec,
        out_shape=jax.ShapeDtypeStruct((nseq, t_new, FW), F32),
        compiler_params=_params(("arbitrary",)),
        name="fox_sample",
    )(page_table, q_s, k_new, v_new, c_new, rep, msuf, msel, lf_t, k_t, v_t)


def _ffn_body(x_ref, om_ref, af_ref, of_ref, buf_ref, wo_ref, gf_ref, wu_ref, wc_ref, bc_ref, wd_ref,
              y_ref, tail_ref, ext, *, tiles_per_seq, t_new):
    tm = x_ref.shape[0]
    if t_new is None:
        @pl.when(pl.program_id(0) % tiles_per_seq == 0)
        def _():
            ext[0:SUB, :] = jnp.zeros((SUB, D_FF), F32)

    mixin = jnp.concatenate([om_ref[...], (of_ref[...] * af_ref[...]).astype(BF16)], axis=1)
    h = x_ref[...] + _dot(mixin, wo_ref[...])
    hn = (h * lax.rsqrt(jnp.mean(h * h, axis=-1, keepdims=True) + EPS) * gf_ref[...]).astype(BF16)
    up_a = _dot(hn, wu_ref[:, 0:D_FF])
    up_g = _dot(hn, wu_ref[:, D_FF:2 * D_FF])
    if t_new is None:
        ext[SUB:SUB + tm, :] = up_a
        conv = wc_ref[0:1, :] * ext[pl.ds(SUB - 2, tm), :]
        for j in range(1, FFN_CONV):
            conv = conv + wc_ref[j:j + 1, :] * ext[pl.ds(SUB - 2 + j, tm), :]
        ext[0:SUB, :] = up_a[tm - SUB:tm, :]
        tail_ref[...] = up_a[tm - SUB:tm, :]
    else:
        nseg = tm // t_new
        cur = up_a.reshape(nseg, t_new, D_FF)
        buf = buf_ref[...].reshape(nseg, t_new, D_FF)
        tpos = lax.broadcasted_iota(jnp.int32, cur.shape, 1)
        w = lambda j: wc_ref[j:j + 1, :].reshape(1, 1, D_FF)
        conv = w(FFN_CONV - 1) * cur
        for d in range(1, FFN_CONV):
            shifted = jnp.where(tpos >= d, pltpu.roll(cur, d, axis=1), pltpu.roll(buf, d, axis=1))
            conv = conv + w(FFN_CONV - 1 - d) * shifted
        conv = conv.reshape(tm, D_FF)
        tail_ref[...] = up_a
    act = (_silu(conv + bc_ref[...]) * up_g).astype(BF16)
    y_ref[...] = h + _dot(act, wd_ref[...])


def _ffn(x2d, outm, af, of, bufpad, w_out, g_ffn, w_up, w_ffconv, b_ffconv, w_down, *, tm, tiles_per_seq, t_new):
    r = x2d.shape[0]
    row = lambda n: pl.BlockSpec((tm, n), lambda i: (i, 0))
    if t_new is None:
        tail_spec = pl.BlockSpec((SUB, D_FF), lambda i: (i, 0))
        tail_shape = jax.ShapeDtypeStruct((r // tm * SUB, D_FF), F32)
        buf_spec = pl.BlockSpec((SUB, D_FF), lambda i: (0, 0))
    else:
        tail_spec = row(D_FF)
        tail_shape = jax.ShapeDtypeStruct((r, D_FF), F32)
        buf_spec = row(D_FF)
    return pl.pallas_call(
        functools.partial(_ffn_body, tiles_per_seq=tiles_per_seq, t_new=t_new),
        grid=(r // tm,),
        in_specs=[row(D_MODEL), row(MW), row(FW), row(FW), buf_spec,
                  _const_spec((D_MODEL, D_MODEL)), _const_spec((1, D_MODEL)), _const_spec((D_MODEL, 2 * D_FF)),
                  _const_spec((FFN_CONV, D_FF)), _const_spec((1, D_FF)), _const_spec((D_FF, D_MODEL))],
        out_specs=[row(D_MODEL), tail_spec],
        out_shape=[jax.ShapeDtypeStruct((r, D_MODEL), F32), tail_shape],
        scratch_shapes=[pltpu.VMEM((SUB + tm, D_FF), F32)],
        compiler_params=_params(("arbitrary",)),
        name="ffn",
    )(x2d, outm, af, of, bufpad, w_out, g_ffn, w_up, w_ffconv, b_ffconv, w_down)


def _lane_pad(v, n=LANES):
    return jnp.pad(v, ((0, 0), (0, n - v.shape[1])))


def _state_rows(state, t_new):
    nseq, k1, c = state.shape
    return jnp.pad(state, ((0, 0), (t_new - k1, 0), (0, 0))).reshape(nseq * t_new, c)


def kernel(x_prompt, x_sample, cache_k, cache_v, cache_logf, page_table, state_C, state_n, state_m, state_qkconv,
           state_ffnconv, g_mix, w_in, b_gate, w_qkconv, g_qf, g_kf, g_mh, w_out, g_ffn, w_up, w_ffconv, b_ffconv,
           w_down):
    assert w_in.shape[0] == 1, "single-layer kernel"
    bsz, seq, _ = x_prompt.shape
    dbsz, t_new, _ = x_sample.shape
    tm = 256
    tm_ffn = 512

    w = w_in[0]
    w_main = w.astype(BF16)
    w_gate = _lane_pad(w[:, C_G:]).astype(BF16)
    bg_pad = _lane_pad(b_gate.astype(F32))
    gq2 = jnp.tile(g_qf.astype(F32), (1, LANES // FD))
    gk2 = jnp.tile(g_kf.astype(F32), (1, LANES // FD))
    gmix = g_mix.astype(F32)
    wconv = w_qkconv[0].astype(F32)
    gmh = g_mh.astype(F32)
    wo_b, wu_b, wd_b = w_out[0].astype(BF16), w_up[0].astype(BF16), w_down[0].astype(BF16)
    gffn, wfc, bfc = g_ffn.astype(F32), w_ffconv[0].astype(F32), b_ffconv.astype(F32)

    ridx = jnp.arange(tm)
    tri_p = (ridx[None, :] <= ridx[:, None]).astype(BF16)
    tri_s = ((ridx[None, :] <= ridx[:, None]) & (ridx[None, :] // t_new == ridx[:, None] // t_new)).astype(BF16)

    xp = x_prompt.reshape(bsz * seq, D_MODEL).astype(F32)
    (qk_p, vm_p, om_p, qa_p, ka_p, va_p, of_p, g_p, _, kt_p, vt_p) = _proj(
        xp, gmix, w_main, w_gate, bg_pad, gq2, gk2, tri_p, tm=tm, tiles_per_seq=seq // tm, kv_transposed=True)
    outm_p, c_p, n_p, m_p = _mlstm_prompt(qk_p, vm_p, om_p, g_p, wconv, gmh, bsz=bsz, seq=seq)
    af_p = _fox_prompt(qa_p, ka_p, va_p, bsz=bsz, seq=seq, tq=512)
    y_p, tail_p = _ffn(xp, outm_p, af_p, of_p, jnp.zeros((SUB, D_FF), F32), wo_b, gffn, wu_b, wfc, bfc, wd_b,
                       tm=tm_ffn, tiles_per_seq=seq // tm_ffn, t_new=None)

    xs = x_sample.reshape(dbsz * t_new, D_MODEL).astype(F32)
    (qk_s, vm_s, om_s, qa_s, _, _, of_s, g_s, fc_s, kf_s, vf_s) = _proj(
        xs, gmix, w_main, w_gate, bg_pad, gq2, gk2, tri_s, tm=tm, tiles_per_seq=1, kv_transposed=False)
    mrows = _lane_pad(jnp.repeat(state_m[0].astype(F32), t_new, axis=0))
    nrows = jnp.repeat(state_n[0].astype(F32).reshape(dbsz, MW), t_new, axis=0)
    n_t = jnp.transpose(state_n[0].astype(F32), (1, 0, 2))
    outm_s, c_s, n_s_t, mrow_s = _mlstm_sample(
        qk_s, _state_rows(state_qkconv[0].astype(F32), t_new), vm_s, om_s, g_s, mrows, nrows,
        state_C[0].astype(F32), n_t, wconv, gmh, t_new=t_new)

    k_new = kf_s.reshape(dbsz, t_new, FH, FD)
    v_new = vf_s.reshape(dbsz, t_new, FH, FD)
    q_s = qa_s.reshape(dbsz, t_new, FH, LANES)[..., :FD].reshape(dbsz, t_new, FW)
    c_new = _lane_pad(jnp.transpose(fc_s.reshape(dbsz, t_new, LANES)[:, :, G_FF:G_FF + FH], (0, 2, 1))
                      .reshape(dbsz * FH, t_new)).reshape(dbsz, FH, LANES)
    npg = page_table.shape[1]
    nrow = FH * t_new
    rr = jnp.arange(nrow)[:, None] // t_new
    cc = jnp.arange(LANES)[None, :]
    rep = ((cc < 3 * FH) & (cc % FH == rr)).astype(BF16)
    kk = jnp.arange(LANES)
    msuf = (kk[:, None] > kk[None, :]).astype(BF16)
    pr = jnp.arange(npg * FH)
    msel = ((pr[None, :] % FH == pr[:, None] % FH) & (pr[None, :] // FH > pr[:, None] // FH)).astype(BF16)
    k_t = jnp.transpose(cache_k[0], (0, 2, 3, 1)).astype(F32)
    v_t = jnp.transpose(cache_v[0], (0, 2, 3, 1)).astype(F32)
    lf_t = jnp.transpose(cache_logf[0], (0, 2, 1)).astype(F32)
    fs_args = (page_table.astype(jnp.int32), q_s, k_new.reshape(dbsz, t_new, FW),
               v_new.reshape(dbsz, t_new, FW), c_new, rep, msuf, msel, lf_t, k_t, v_t)
    af_s = _fox_sample(*fs_args, ch=16, nslot=3)
    y_s, tail_s = _ffn(xs, outm_s, af_s.reshape(dbsz * t_new, FW), of_s,
                       _state_rows(state_ffnconv[0].astype(F32), t_new), wo_b, gffn, wu_b, wfc, bfc, wd_b,
                       tm=tm, tiles_per_seq=1, t_new=t_new)

    dt = x_prompt.dtype
    y_prompt = y_p.reshape(bsz, seq, D_MODEL).astype(dt)
    y_sample = y_s.reshape(dbsz, t_new, D_MODEL).astype(x_sample.dtype)
    k_prompt = jnp.transpose(kt_p, (0, 3, 1, 2))[None]
    v_prompt = jnp.transpose(vt_p, (0, 3, 1, 2))[None]
    logf_prompt = g_p.reshape(bsz, seq, LANES)[:, :, G_FF:G_FF + FH][None]
    c_prompt = c_p[None]
    n_prompt = n_p[:, :MH, :][None]
    m_prompt = m_p[:, :MH, 0][None]
    qkconv_prompt = qk_p.reshape(bsz, seq, 2 * MW)[:, seq - (QK_CONV - 1):, :][None]
    ffnconv_prompt = tail_p.reshape(bsz, seq // tm_ffn, SUB, D_FF)[:, -1, SUB - (FFN_CONV - 1):, :][None]
    k_sample = k_new[None]
    v_sample = v_new[None]
    logf_sample = g_s.reshape(dbsz, t_new, LANES)[:, :, G_FF:G_FF + FH][None]
    c_sample = c_s[None]
    n_sample = jnp.transpose(n_s_t, (1, 0, 2))[None]
    m_sample = mrow_s.reshape(dbsz, t_new, LANES)[:, t_new - 1, :MH][None]
    qkconv_sample = qk_s.reshape(dbsz, t_new, 2 * MW)[:, t_new - (QK_CONV - 1):, :][None]
    ffnconv_sample = tail_s.reshape(dbsz, t_new, D_FF)[:, t_new - (FFN_CONV - 1):, :][None]
    return (y_prompt, y_sample, k_prompt, v_prompt, logf_prompt, c_prompt, n_prompt, m_prompt, qkconv_prompt,
            ffnconv_prompt, k_sample, v_sample, logf_sample, c_sample, n_sample, m_sample, qkconv_sample,
            ffnconv_sample)
```
